```python
import math
import jax, jax.numpy as jnp
from jax import lax
import numpy as np

D_MODEL = 1024
BATCH = 8
SEQ = 4096
DEPTH = 2

PLE_DIM = 256
N_MIXERS = 2
N_GLA_LAYERS = (DEPTH + 1) // 2
N_SWA_LAYERS = DEPTH // 2

GLA_HEADS = 4
GLA_DK = D_MODEL // 2 // GLA_HEADS
GLA_DV = D_MODEL // GLA_HEADS
GLA_LOWRANK = 16
GLA_TAU = 16.0
GLA_CHUNK = 64
GLA_IN_COLS = 2 * GLA_HEADS * GLA_DK + 2 * GLA_HEADS * GLA_DV + GLA_LOWRANK

SWA_HEAD_DIM = 64
SWA_Q_HEADS = D_MODEL // SWA_HEAD_DIM
SWA_KV_HEADS = 4
SWA_WINDOW = 128
SWA_BLOCK = 128
SWA_IN_COLS = (SWA_Q_HEADS + 2 * SWA_KV_HEADS) * SWA_HEAD_DIM

D_FF = 4 * D_MODEL

DEEPNORM_ALPHA = (2.0 * DEPTH) ** 0.25
DEEPNORM_BETA = (8.0 * DEPTH) ** -0.25
LN_EPS = 1e-5
RMS_EPS = 1e-5

kernel_name = 'hybrid_gla_swa_sink_deepnorm'


def layer_norm(x, g, b):
    xf = x.astype(jnp.float32)
    mu = jnp.mean(xf, axis=-1, keepdims=True)
    var = jnp.mean(jnp.square(xf - mu), axis=-1, keepdims=True)
    return ((xf - mu) * lax.rsqrt(var + LN_EPS) * g.astype(jnp.float32) + b.astype(jnp.float32)).astype(x.dtype)


def gla_mixer(x, w_in, w_gk_up, b_gk, norm_g, w_out):
    B, S, _ = x.shape
    H, DK, DV, C = GLA_HEADS, GLA_DK, GLA_DV, GLA_CHUNK
    nc = S // C
    f32 = jnp.float32
    proj = x @ w_in
    q, k, v, r, gk_low = jnp.split(
        proj, [H * DK, 2 * H * DK, 2 * H * DK + H * DV, 2 * H * DK + 2 * H * DV], axis=-1)
    log_a = jax.nn.log_sigmoid((gk_low @ w_gk_up + b_gk).astype(f32)) / GLA_TAU

    def to_chunks(t, d):
        return t.astype(f32).reshape(B, nc, C, H, d).transpose(1, 0, 3, 2, 4)

    qc = to_chunks(q, DK) * (DK ** -0.5)
    kc = to_chunks(k, DK)
    vc = to_chunks(v, DV)
    gc = to_chunks(log_a, DK)
    causal = jnp.tril(jnp.ones((C, C), dtype=bool))[:, :, None]

    def step(state, inp):
        qb, kb, vb, gb = inp
        bcum = jnp.cumsum(gb, axis=2)
        b_last = bcum[:, :, -1:, :]
        o_inter = jnp.einsum('bhcd,bhde->bhce', qb * jnp.exp(bcum), state)
        diff = bcum[:, :, :, None, :] - bcum[:, :, None, :, :]
        decay = jnp.exp(jnp.where(causal, diff, -jnp.inf))
        attn = jnp.einsum('bhid,bhjd,bhijd->bhij', qb, kb, decay)
        o = o_inter + jnp.einsum('bhij,bhje->bhie', attn, vb)
        new_state = jnp.exp(b_last[:, :, 0, :])[..., None] * state + jnp.einsum(
            'bhcd,bhce->bhde', kb * jnp.exp(b_last - bcum), vb)
        return new_state, o

    state0 = jnp.zeros((B, H, DK, DV), f32)
    _, oc = lax.scan(step, state0, (qc, kc, vc, gc))
    o = oc.transpose(1, 0, 3, 2, 4).reshape(B, S, H, DV)
    o = o * lax.rsqrt(jnp.mean(o * o, axis=-1, keepdims=True) + RMS_EPS) * norm_g.astype(f32)
    o = o.reshape(B, S, H * DV) * jax.nn.silu(r.astype(f32))
    return o.astype(x.dtype) @ w_out


def swa_mixer(x, w_qkv, b_qkv, sinks, w_out, b_out):
    B, S, _ = x.shape
    Hq, Hkv, hd, BLK = SWA_Q_HEADS, SWA_KV_HEADS, SWA_HEAD_DIM, SWA_BLOCK
    G = Hq // Hkv
    nb = S // BLK
    f32 = jnp.float32
    qkv = x @ w_qkv + b_qkv
    q, k, v = jnp.split(qkv, [Hq * hd, (Hq + Hkv) * hd], axis=-1)
    q = q.astype(f32).reshape(B, nb, BLK, Hkv, G, hd)
    k = k.astype(f32).reshape(B, nb, BLK, Hkv, hd)
    v = v.astype(f32).reshape(B, nb, BLK, Hkv, hd)
    pad = ((0, 0), (1, 0), (0, 0), (0, 0), (0, 0))
    k_band = jnp.concatenate([jnp.pad(k, pad)[:, :-1], k], axis=2)
    v_band = jnp.concatenate([jnp.pad(v, pad)[:, :-1], v], axis=2)
    scores = jnp.einsum('bnqhgd,bnjhd->bnhgqj', q, k_band) * (hd ** -0.5)
    qi = jnp.arange(BLK)[:, None]
    jj = jnp.arange(2 * BLK)[None, :]
    rel = jj - BLK - qi
    valid = (rel <= 0) & (rel > -SWA_WINDOW)
    scores = jnp.where(valid, scores, -jnp.inf)
    sink = sinks.astype(f32).reshape(Hkv, G)[None, None, :, :, None, None]
    m = jnp.maximum(jnp.max(scores, axis=-1, keepdims=True), sink)
    pr = jnp.exp(scores - m)
    denom = jnp.sum(pr, axis=-1, keepdims=True) + jnp.exp(sink - m)
    out = jnp.einsum('bnhgqj,bnjhd->bnqhgd', pr / denom, v_band)
    out = out.reshape(B, S, Hq * hd).astype(x.dtype)
    return out @ w_out + b_out


def sqrelu_mlp(x, w_up, w_down):
    h = jax.nn.relu(x @ w_up)
    return (h * h) @ w_down


def setup_inputs(seed: int = 0) -> dict:
    key = jax.random.key(seed)
    ks = jax.random.split(key, 22)
    nrm = jax.random.normal
    f32 = jnp.float32
    D = D_MODEL
    x = nrm(ks[0], (BATCH, SEQ, D), f32)
    p = nrm(ks[1], (DEPTH, BATCH, SEQ, PLE_DIM), f32)
    gla_w_in = nrm(ks[2], (N_GLA_LAYERS, D, GLA_IN_COLS), f32) * D ** -0.5
    gla_w_gk_up = nrm(ks[3], (N_GLA_LAYERS, GLA_LOWRANK, GLA_HEADS * GLA_DK), f32) * GLA_LOWRANK ** -0.5
    gla_b_gk = 0.1 * nrm(ks[4], (N_GLA_LAYERS, GLA_HEADS * GLA_DK), f32)
    gla_norm_g = 1.0 + 0.02 * nrm(ks[5], (N_GLA_LAYERS, GLA_DV), f32)
    gla_w_out = nrm(ks[6], (N_GLA_LAYERS, GLA_HEADS * GLA_DV, D), f32) * (GLA_HEADS * GLA_DV) ** -0.5 * DEEPNORM_BETA
    swa_w_qkv = nrm(ks[7], (N_SWA_LAYERS, D, SWA_IN_COLS), f32) * D ** -0.5
    swa_b_qkv = 0.02 * nrm(ks[8], (N_SWA_LAYERS, SWA_IN_COLS), f32)
    swa_sinks = 0.5 * nrm(ks[9], (N_SWA_LAYERS, SWA_Q_HEADS), f32)
    swa_w_out = nrm(ks[10], (N_SWA_LAYERS, SWA_Q_HEADS * SWA_HEAD_DIM, D), f32) * (SWA_Q_HEADS * SWA_HEAD_DIM) ** -0.5 * DEEPNORM_BETA
    swa_b_out = 0.02 * nrm(ks[11], (N_SWA_LAYERS, D), f32)
    mlp_w_up = nrm(ks[12], (DEPTH, D, D_FF), f32) * D ** -0.5
    mlp_w_down = nrm(ks[13], (DEPTH, D_FF, D), f32) * D_FF ** -0.5 * DEEPNORM_BETA
    ln1_g = 1.0 + 0.02 * nrm(ks[14], (DEPTH, D), f32)
    ln1_b = 0.02 * nrm(ks[15], (DEPTH, D), f32)
    ln2_g = 1.0 + 0.02 * nrm(ks[16], (DEPTH, D), f32)
    ln2_b = 0.02 * nrm(ks[17], (DEPTH, D), f32)
    ple_w_proj = nrm(ks[18], (DEPTH, PLE_DIM, D), f32) * PLE_DIM ** -0.5
    ple_w_gate = nrm(ks[19], (DEPTH, D, D), f32) * D ** -0.5
    ple_b_gate = 0.02 * nrm(ks[20], (DEPTH, D), f32)
    return {'x': x, 'p': p,
            'gla_w_in': gla_w_in, 'gla_w_gk_up': gla_w_gk_up, 'gla_b_gk': gla_b_gk,
            'gla_norm_g': gla_norm_g, 'gla_w_out': gla_w_out,
            'swa_w_qkv': swa_w_qkv, 'swa_b_qkv': swa_b_qkv, 'swa_sinks': swa_sinks,
            'swa_w_out': swa_w_out, 'swa_b_out': swa_b_out,
            'mlp_w_up': mlp_w_up, 'mlp_w_down': mlp_w_down,
            'ln1_g': ln1_g, 'ln1_b': ln1_b, 'ln2_g': ln2_g, 'ln2_b': ln2_b,
            'ple_w_proj': ple_w_proj, 'ple_w_gate': ple_w_gate, 'ple_b_gate': ple_b_gate}


def reference(x, p, gla_w_in, gla_w_gk_up, gla_b_gk, gla_norm_g, gla_w_out,
              swa_w_qkv, swa_b_qkv, swa_sinks, swa_w_out, swa_b_out,
              mlp_w_up, mlp_w_down, ln1_g, ln1_b, ln2_g, ln2_b,
              ple_w_proj, ple_w_gate, ple_b_gate):
    h = x
    for i in range(DEPTH):
        j = i // N_MIXERS
        if i % N_MIXERS == 0:
            mix = gla_mixer(h, gla_w_in[j], gla_w_gk_up[j], gla_b_gk[j], gla_norm_g[j], gla_w_out[j])
        else:
            mix = swa_mixer(h, swa_w_qkv[j], swa_b_qkv[j], swa_sinks[j], swa_w_out[j], swa_b_out[j])
        h = layer_norm(DEEPNORM_ALPHA * h + mix, ln1_g[i], ln1_b[i])
        h = layer_norm(DEEPNORM_ALPHA * h + sqrelu_mlp(h, mlp_w_up[i], mlp_w_down[i]), ln2_g[i], ln2_b[i])
        gate = jax.nn.sigmoid(h @ ple_w_gate[i] + ple_b_gate[i])
        h = h + gate * (p[i] @ ple_w_proj[i])
    return h
```

```python
import functools

import jax
import jax.numpy as jnp
from jax import lax
from jax.experimental import pallas as pl
from jax.experimental.pallas import tpu as pltpu

D_MODEL = 1024
DEPTH = 2
PLE_DIM = 256

GLA_HEADS = 4
GLA_DK = 128
GLA_DV = 256
GLA_LOWRANK = 16
GLA_TAU = 16.0
GLA_QK = GLA_HEADS * GLA_DK
GLA_V = GLA_HEADS * GLA_DV

SWA_HEAD_DIM = 64
SWA_Q_HEADS = 16
SWA_KV_HEADS = 4
SWA_GROUP = SWA_Q_HEADS // SWA_KV_HEADS
SWA_BLOCK = 128
SWA_KV = SWA_KV_HEADS * SWA_HEAD_DIM

D_FF = 4 * D_MODEL
DEEPNORM_ALPHA = (2.0 * DEPTH) ** 0.25
LN_EPS = 1e-5
RMS_EPS = 1e-5

LANES = 128
GLA_CHUNK = 128
ROW_TILE = 512
FF_TILE = 1024
VMEM_LIMIT = 56 * 1024 * 1024

BF16 = jnp.bfloat16
F32 = jnp.float32


def _dot(a, b):
    return jnp.dot(a, b, preferred_element_type=F32)


def _dot_nt(a, b):
    return lax.dot_general(a, b, (((1,), (1,)), ((), ())), preferred_element_type=F32)


def _resident(shape):
    nd = len(shape)
    return pl.BlockSpec(shape, lambda *_: (0,) * nd, pipeline_mode=pl.Buffered(1))


def _layer_norm(y, g, b):
    mu = jnp.mean(y, axis=-1, keepdims=True)
    yc = y - mu
    var = jnp.mean(yc * yc, axis=-1, keepdims=True)
    return yc * lax.rsqrt(var + LN_EPS) * g + b


def _gla_inproj_kernel(x_ref, wq_ref, wk_ref, wv_ref, wr_ref, wgl_ref, wup_ref, bgk_ref,
                       q_ref, k_ref, v_ref, r_ref, bcum_ref):
    xb = x_ref[...].astype(BF16)
    q_ref[...] = (_dot(xb, wq_ref[...]) * (GLA_DK ** -0.5)).astype(BF16)
    k_ref[...] = _dot(xb, wk_ref[...]).astype(BF16)
    v_ref[...] = _dot(xb, wv_ref[...]).astype(BF16)
    r_ref[...] = _dot(xb, wr_ref[...]).astype(BF16)
    gl = _dot(xb, wgl_ref[...]).astype(BF16)
    z = _dot(gl, wup_ref[...]) + bgk_ref[...]
    log_a = (jnp.minimum(z, 0.0) - jnp.log1p(jnp.exp(-jnp.abs(z)))) * (1.0 / GLA_TAU)
    row = lax.broadcasted_iota(jnp.int32, (GLA_CHUNK, GLA_CHUNK), 0)
    col = lax.broadcasted_iota(jnp.int32, (GLA_CHUNK, GLA_CHUNK), 1)
    tril = (col <= row).astype(BF16)
    for c in range(ROW_TILE // GLA_CHUNK):
        la = log_a[c * GLA_CHUNK:(c + 1) * GLA_CHUNK]
        hi = la.astype(BF16)
        rem = la - hi.astype(F32)
        mid = rem.astype(BF16)
        lo = (rem - mid.astype(F32)).astype(BF16)
        bcum_ref[c * GLA_CHUNK:(c + 1) * GLA_CHUNK, :] = (
            _dot(tril, hi) + _dot(tril, mid) + _dot(tril, lo))


def _gla_inproj(x2, wq, wk, wv, wr, wgl, wup, bgk):
    t = x2.shape[0]
    row = lambda w: pl.BlockSpec((ROW_TILE, w), lambda i: (i, 0))
    return pl.pallas_call(
        _gla_inproj_kernel,
        grid=(t // ROW_TILE,),
        in_specs=[row(D_MODEL), _resident(wq.shape), _resident(wk.shape), _resident(wv.shape),
                  _resident(wr.shape), _resident(wgl.shape), _resident(wup.shape),
                  _resident(bgk.shape)],
        out_specs=[row(GLA_QK), row(GLA_QK), row(GLA_V), row(GLA_V), row(GLA_QK)],
        out_shape=[jax.ShapeDtypeStruct((t, GLA_QK), BF16),
                   jax.ShapeDtypeStruct((t, GLA_QK), BF16),
                   jax.ShapeDtypeStruct((t, GLA_V), BF16),
                   jax.ShapeDtypeStruct((t, GLA_V), BF16),
                   jax.ShapeDtypeStruct((t, GLA_QK), F32)],
        compiler_params=pltpu.CompilerParams(
            dimension_semantics=("parallel",), vmem_limit_bytes=VMEM_LIMIT),
        name="gla_inproj",
    )(x2, wq, wk, wv, wr, wgl, wup, bgk)


def _gla_chunk_kernel(q_ref, k_ref, v_ref, r_ref, b_ref, ng_ref, o_ref, state_ref):
    @pl.when(pl.program_id(1) == 0)
    def _():
        state_ref[...] = jnp.zeros_like(state_ref)

    row = lax.broadcasted_iota(jnp.int32, (GLA_CHUNK, GLA_CHUNK), 0)
    col = lax.broadcasted_iota(jnp.int32, (GLA_CHUNK, GLA_CHUNK), 1)
    causal = col <= row
    ng = ng_ref[...]
    for h in range(GLA_HEADS):
        ks = slice(h * GLA_DK, (h + 1) * GLA_DK)
        vs = slice(h * GLA_DV, (h + 1) * GLA_DV)
        b = b_ref[:, ks]
        b_last = b[GLA_CHUNK - 1:GLA_CHUNK, :]
        q = q_ref[:, ks].astype(F32)
        k = k_ref[:, ks].astype(F32)
        v = v_ref[:, vs]
        qd = (q * jnp.exp(b)).astype(BF16)
        kd = (k * jnp.exp(-b)).astype(BF16)
        kl = k * jnp.exp(b_last - b)
        att = jnp.where(causal, _dot_nt(qd, kd), 0.0).astype(BF16)
        s_prev = state_ref[h]
        o = _dot(att, v) + _dot(qd, s_prev.astype(BF16))
        upd = _dot(kl.T.astype(BF16), v)
        decay = jnp.broadcast_to(jnp.exp(b_last), (GLA_CHUNK, GLA_DK)).T
        decay = jnp.concatenate([decay] * (GLA_DV // GLA_CHUNK), axis=1)
        state_ref[h] = decay * s_prev + upd
        o = o * lax.rsqrt(jnp.mean(o * o, axis=-1, keepdims=True) + RMS_EPS) * ng
        r = r_ref[:, vs].astype(F32)
        o_ref[:, vs] = (o * (r / (1.0 + jnp.exp(-r)))).astype(BF16)


def _gla_chunk(q, k, v, r, bcum, norm_g, batch, seq):
    nc = seq // GLA_CHUNK
    blk = lambda w: pl.BlockSpec((GLA_CHUNK, w), lambda b, c: (b * nc + c, 0))
    return pl.pallas_call(
        _gla_chunk_kernel,
        grid=(batch, nc),
        in_specs=[blk(GLA_QK), blk(GLA_QK), blk(GLA_V), blk(GLA_V), blk(GLA_QK),
                  pl.BlockSpec((1, GLA_DV), lambda b, c: (0, 0))],
        out_specs=blk(GLA_V),
        out_shape=jax.ShapeDtypeStruct((batch * seq, GLA_V), BF16),
        scratch_shapes=[pltpu.VMEM((GLA_HEADS, GLA_DK, GLA_DV), F32)],
        compiler_params=pltpu.CompilerParams(
            dimension_semantics=("parallel", "arbitrary"), vmem_limit_bytes=VMEM_LIMIT),
        name="gla_chunk",
    )(q, k, v, r, bcum, norm_g)


def _swa_qkv_kernel(h_ref, wq_ref, wk_ref, wv_ref, bq_ref, bk_ref, bv_ref, q_ref, k_ref, v_ref):
    hb = h_ref[...].astype(BF16)
    q_ref[...] = ((_dot(hb, wq_ref[...]) + bq_ref[...]) * (SWA_HEAD_DIM ** -0.5)).astype(BF16)
    k_ref[...] = (_dot(hb, wk_ref[...]) + bk_ref[...]).astype(BF16)
    v_ref[...] = (_dot(hb, wv_ref[...]) + bv_ref[...]).astype(BF16)


def _swa_qkv(h2, wq, wk, wv, bq, bk, bv):
    t = h2.shape[0]
    row = lambda w: pl.BlockSpec((ROW_TILE, w), lambda i: (i, 0))
    return pl.pallas_call(
        _swa_qkv_kernel,
        grid=(t // ROW_TILE,),
        in_specs=[row(D_MODEL), _resident(wq.shape), _resident(wk.shape), _resident(wv.shape),
                  _resident(bq.shape), _resident(bk.shape), _resident(bv.shape)],
        out_specs=[row(D_MODEL), row(SWA_KV), row(SWA_KV)],
        out_shape=[jax.ShapeDtypeStruct((t, D_MODEL), BF16),
                   jax.ShapeDtypeStruct((t, SWA_KV), BF16),
                   jax.ShapeDtypeStruct((t, SWA_KV), BF16)],
        compiler_params=pltpu.CompilerParams(
            dimension_semantics=("parallel",), vmem_limit_bytes=VMEM_LIMIT),
        name="swa_qkv",
    )(h2, wq, wk, wv, bq, bk, bv)


def _swa_attn_kernel(sink_ref, q_ref, kp_ref, kc_ref, vp_ref, vc_ref, o_ref):
    has_prev = (pl.program_id(1) > 0).astype(BF16)
    k_band = jnp.concatenate([kp_ref[...] * has_prev, kc_ref[...]], axis=0)
    v_band = jnp.concatenate([vp_ref[...] * has_prev, vc_ref[...]], axis=0)
    qi = lax.broadcasted_iota(jnp.int32, (SWA_BLOCK, 2 * SWA_BLOCK), 0)
    jj = lax.broadcasted_iota(jnp.int32, (SWA_BLOCK, 2 * SWA_BLOCK), 1)
    rel = jj - SWA_BLOCK - qi
    valid = (rel <= 0) & (rel > -SWA_BLOCK)
    for kv in range(SWA_KV_HEADS):
        cs = slice(kv * SWA_HEAD_DIM, (kv + 1) * SWA_HEAD_DIM)
        kh = k_band[:, cs]
        vh = v_band[:, cs]
        for g in range(SWA_GROUP):
            head = kv * SWA_GROUP + g
            hs = slice(head * SWA_HEAD_DIM, (head + 1) * SWA_HEAD_DIM)
            s = jnp.where(valid, _dot_nt(q_ref[:, hs], kh), -jnp.inf)
            sink = sink_ref[head]
            m = jnp.maximum(jnp.max(s, axis=-1, keepdims=True), sink)
            p = jnp.exp(s - m)
            denom = jnp.sum(p, axis=-1, keepdims=True) + jnp.exp(sink - m)
            o_ref[:, hs] = (_dot(p.astype(BF16), vh) / denom).astype(BF16)


def _swa_attn(sinks, q, k, v, batch, seq):
    nb = seq // SWA_BLOCK
    cur = lambda w: pl.BlockSpec((SWA_BLOCK, w), lambda b, n, *_: (b * nb + n, 0))
    prev = lambda w: pl.BlockSpec(
        (SWA_BLOCK, w), lambda b, n, *_: (b * nb + jnp.maximum(n - 1, 0), 0))
    return pl.pallas_call(
        _swa_attn_kernel,
        grid_spec=pltpu.PrefetchScalarGridSpec(
            num_scalar_prefetch=1,
            grid=(batch, nb),
            in_specs=[cur(D_MODEL), prev(SWA_KV), cur(SWA_KV), prev(SWA_KV), cur(SWA_KV)],
            out_specs=cur(D_MODEL)),
        out_shape=jax.ShapeDtypeStruct((batch * seq, D_MODEL), BF16),
        compiler_params=pltpu.CompilerParams(
            dimension_semantics=("parallel", "parallel"), vmem_limit_bytes=VMEM_LIMIT),
        name="swa_attn",
    )(sinks, q, k, k, v, v)


def _post_mixer_kernel(mix_ref, h_ref, p_ref, wo_ref, bo_ref, g1_ref, b1_ref, wup_ref, wdn_ref,
                       g2_ref, b2_ref, wg_ref, bg_ref, wp_ref, out_ref):
    y = DEEPNORM_ALPHA * h_ref[...] + _dot(mix_ref[...], wo_ref[...]) + bo_ref[...]
    h1 = _layer_norm(y, g1_ref[...], b1_ref[...])
    h1b = h1.astype(BF16)
    acc = DEEPNORM_ALPHA * h1
    for c in range(D_FF // FF_TILE):
        u = jnp.maximum(_dot(h1b, wup_ref[:, c * FF_TILE:(c + 1) * FF_TILE]), 0.0)
        acc = acc + _dot((u * u).astype(BF16), wdn_ref[c * FF_TILE:(c + 1) * FF_TILE, :])
    h2 = _layer_norm(acc, g2_ref[...], b2_ref[...])
    z = _dot(h2.astype(BF16), wg_ref[...]) + bg_ref[...]
    gate = 1.0 / (1.0 + jnp.exp(-z))
    out_ref[...] = h2 + gate * _dot(p_ref[...].astype(BF16), wp_ref[...])


def _post_mixer(mix, h2d, p3, layer, wo, bo, g1, b1, wup, wdn, g2, b2, wg, bg, wp):
    t = h2d.shape[0]
    row = lambda w: pl.BlockSpec((ROW_TILE, w), lambda i: (i, 0))
    weights = (wo, bo, g1, b1, wup, wdn, g2, b2, wg, bg, wp)
    return pl.pallas_call(
        _post_mixer_kernel,
        grid=(t // ROW_TILE,),
        in_specs=[row(D_MODEL), row(D_MODEL),
                  pl.BlockSpec((None, ROW_TILE, PLE_DIM), lambda i: (layer, i, 0))]
                 + [_resident(w.shape) for w in weights],
        out_specs=row(D_MODEL),
        out_shape=jax.ShapeDtypeStruct((t, D_MODEL), F32),
        compiler_params=pltpu.CompilerParams(
            dimension_semantics=("parallel",), vmem_limit_bytes=VMEM_LIMIT),
        name=f"post_mixer_{layer}",
    )(mix, h2d, p3, *weights)


def kernel(x, p, gla_w_in, gla_w_gk_up, gla_b_gk, gla_norm_g, gla_w_out, swa_w_qkv, swa_b_qkv,
           swa_sinks, swa_w_out, swa_b_out, mlp_w_up, mlp_w_down, ln1_g, ln1_b, ln2_g, ln2_b,
           ple_w_proj, ple_w_gate, ple_b_gate):
    batch, seq, d = x.shape
    t = batch * seq
    x2 = x.reshape(t, d)
    p3 = p.reshape(DEPTH, t, PLE_DIM)
    vec = lambda a: a.reshape(1, -1).astype(F32)

    def post(mix, h, layer, wo, bo):
        return _post_mixer(
            mix, h, p3, layer, wo.astype(BF16), vec(bo), vec(ln1_g[layer]), vec(ln1_b[layer]),
            mlp_w_up[layer].astype(BF16), mlp_w_down[layer].astype(BF16),
            vec(ln2_g[layer]), vec(ln2_b[layer]), ple_w_gate[layer].astype(BF16),
            vec(ple_b_gate[layer]), ple_w_proj[layer].astype(BF16))

    w_in = gla_w_in[0].astype(BF16)
    c0, c1, c2, c3 = GLA_QK, 2 * GLA_QK, 2 * GLA_QK + GLA_V, 2 * GLA_QK + 2 * GLA_V
    wgl = jnp.pad(w_in[:, c3:], ((0, 0), (0, LANES - GLA_LOWRANK)))
    wup = jnp.pad(gla_w_gk_up[0].astype(BF16), ((0, LANES - GLA_LOWRANK), (0, 0)))
    q, k, v, r, bcum = _gla_inproj(x2, w_in[:, :c0], w_in[:, c0:c1], w_in[:, c1:c2],
                                   w_in[:, c2:c3], wgl, wup, vec(gla_b_gk[0]))
    mix = _gla_chunk(q, k, v, r, bcum, vec(gla_norm_g[0]), batch, seq)
    h = post(mix, x2, 0, gla_w_out[0], jnp.zeros((d,), F32))

    w_qkv = swa_w_qkv[0].astype(BF16)
    b_qkv = swa_b_qkv[0]
    q, k, v = _swa_qkv(h, w_qkv[:, :D_MODEL], w_qkv[:, D_MODEL:D_MODEL + SWA_KV],
                       w_qkv[:, D_MODEL + SWA_KV:], vec(b_qkv[:D_MODEL]),
                       vec(b_qkv[D_MODEL:D_MODEL + SWA_KV]), vec(b_qkv[D_MODEL + SWA_KV:]))
    mix = _swa_attn(swa_sinks[0].astype(F32), q, k, v, batch, seq)
    h = post(mix, h, 1, swa_w_out[0], swa_b_out[0])
    return h.reshape(batch, seq, d)
```

```python
import functools

import jax
import jax.numpy as jnp
from jax import lax
from jax.experimental import pallas as pl
from jax.experimental.pallas import tpu as pltpu

D_MODEL = 1024
DEPTH = 2
PLE_DIM = 256

GLA_HEADS = 4
GLA_DK = 128
GLA_DV = 256
GLA_LOWRANK = 16
GLA_TAU = 16.0
GLA_QK = GLA_HEADS * GLA_DK
GLA_V = GLA_HEADS * GLA_DV

SWA_HEAD_DIM = 64
SWA_Q_HEADS = 16
SWA_KV_HEADS = 4
SWA_GROUP = SWA_Q_HEADS // SWA_KV_HEADS
SWA_BLOCK = 128
SWA_Q_TILE = 256
SWA_KV = SWA_KV_HEADS * SWA_HEAD_DIM

D_FF = 4 * D_MODEL
DEEPNORM_ALPHA = (2.0 * DEPTH) ** 0.25
LN_EPS = 1e-5
RMS_EPS = 1e-5

LANES = 128
GLA_CHUNK = 128
ROW_TILE = 512
FF_TILE = 1024
VMEM_LIMIT = 56 * 1024 * 1024

BF16 = jnp.bfloat16
F32 = jnp.float32


def _dot(a, b):
    return jnp.dot(a, b, preferred_element_type=F32)


def _dot_nt(a, b):
    return lax.dot_general(a, b, (((1,), (1,)), ((), ())), preferred_element_type=F32)


def _resident(shape):
    nd = len(shape)
    return pl.BlockSpec(shape, lambda *_: (0,) * nd, pipeline_mode=pl.Buffered(1))


def _layer_norm(y, g, b):
    mu = jnp.mean(y, axis=-1, keepdims=True)
    yc = y - mu
    var = jnp.mean(yc * yc, axis=-1, keepdims=True)
    return yc * lax.rsqrt(var + LN_EPS) * g + b


def _gla_inproj_kernel(x_ref, wq_ref, wk_ref, wv_ref, wr_ref, wgl_ref, wup_ref, bgk_ref,
                       q_ref, k_ref, v_ref, r_ref, bcum_ref):
    xb = x_ref[...].astype(BF16)
    q_ref[...] = (_dot(xb, wq_ref[...]) * (GLA_DK ** -0.5)).astype(BF16)
    k_ref[...] = _dot(xb, wk_ref[...]).astype(BF16)
    v_ref[...] = _dot(xb, wv_ref[...]).astype(BF16)
    r_ref[...] = _dot(xb, wr_ref[...]).astype(BF16)
    gl = _dot(xb, wgl_ref[...]).astype(BF16)
    z = _dot(gl, wup_ref[...]) + bgk_ref[...]
    log_a = (jnp.minimum(z, 0.0) - jnp.log1p(jnp.exp(-jnp.abs(z)))) * (1.0 / GLA_TAU)
    row = lax.broadcasted_iota(jnp.int32, (GLA_CHUNK, GLA_CHUNK), 0)
    col = lax.broadcasted_iota(jnp.int32, (GLA_CHUNK, GLA_CHUNK), 1)
    tril = (col <= row).astype(BF16)
    for c in range(ROW_TILE // GLA_CHUNK):
        la = log_a[c * GLA_CHUNK:(c + 1) * GLA_CHUNK]
        hi = la.astype(BF16)
        rem = la - hi.astype(F32)
        mid = rem.astype(BF16)
        lo = (rem - mid.astype(F32)).astype(BF16)
        bcum_ref[c * GLA_CHUNK:(c + 1) * GLA_CHUNK, :] = (
            _dot(tril, hi) + _dot(tril, mid) + _dot(tril, lo))


def _gla_inproj(x2, wq, wk, wv, wr, wgl, wup, bgk):
    t = x2.shape[0]
    row = lambda w: pl.BlockSpec((ROW_TILE, w), lambda i: (i, 0))
    return pl.pallas_call(
        _gla_inproj_kernel,
        grid=(t // ROW_TILE,),
        in_specs=[row(D_MODEL), _resident(wq.shape), _resident(wk.shape), _resident(wv.shape),
                  _resident(wr.shape), _resident(wgl.shape), _resident(wup.shape),
                  _resident(bgk.shape)],
        out_specs=[row(GLA_QK), row(GLA_QK), row(GLA_V), row(GLA_V), row(GLA_QK)],
        out_shape=[jax.ShapeDtypeStruct((t, GLA_QK), BF16),
                   jax.ShapeDtypeStruct((t, GLA_QK), BF16),
                   jax.ShapeDtypeStruct((t, GLA_V), BF16),
                   jax.ShapeDtypeStruct((t, GLA_V), BF16),
                   jax.ShapeDtypeStruct((t, GLA_QK), F32)],
        compiler_params=pltpu.CompilerParams(
            dimension_semantics=("parallel",), vmem_limit_bytes=VMEM_LIMIT),
        name="gla_inproj",
    )(x2, wq, wk, wv, wr, wgl, wup, bgk)


def _gla_chunk_kernel(q_ref, k_ref, v_ref, r_ref, b_ref, ng_ref, o_ref, state_ref):
    @pl.when(pl.program_id(1) == 0)
    def _():
        state_ref[...] = jnp.zeros_like(state_ref)

    row = lax.broadcasted_iota(jnp.int32, (GLA_CHUNK, GLA_CHUNK), 0)
    col = lax.broadcasted_iota(jnp.int32, (GLA_CHUNK, GLA_CHUNK), 1)
    causal = col <= row
    ng = ng_ref[...]
    for h in range(GLA_HEADS):
        ks = slice(h * GLA_DK, (h + 1) * GLA_DK)
        vs = slice(h * GLA_DV, (h + 1) * GLA_DV)
        b = b_ref[:, ks]
        b_last = b[GLA_CHUNK - 1:GLA_CHUNK, :]
        q = q_ref[:, ks].astype(F32)
        k = k_ref[:, ks].astype(F32)
        v = v_ref[:, vs]
        qd = (q * jnp.exp(b)).astype(BF16)
        kd = (k * jnp.exp(-b)).astype(BF16)
        kl = k * jnp.exp(b_last - b)
        att = jnp.where(causal, _dot_nt(qd, kd), 0.0).astype(BF16)
        s_prev = state_ref[h]
        o = _dot(att, v) + _dot(qd, s_prev.astype(BF16))
        upd = _dot(kl.T.astype(BF16), v)
        decay = jnp.broadcast_to(jnp.exp(b_last), (GLA_CHUNK, GLA_DK)).T
        decay = jnp.concatenate([decay] * (GLA_DV // GLA_CHUNK), axis=1)
        state_ref[h] = decay * s_prev + upd
        o = o * lax.rsqrt(jnp.mean(o * o, axis=-1, keepdims=True) + RMS_EPS) * ng
        r = r_ref[:, vs].astype(F32)
        o_ref[:, vs] = (o * (r / (1.0 + jnp.exp(-r)))).astype(BF16)


def _gla_chunk(q, k, v, r, bcum, norm_g, batch, seq):
    nc = seq // GLA_CHUNK
    blk = lambda w: pl.BlockSpec((GLA_CHUNK, w), lambda b, c: (b * nc + c, 0))
    return pl.pallas_call(
        _gla_chunk_kernel,
        grid=(batch, nc),
        in_specs=[blk(GLA_QK), blk(GLA_QK), blk(GLA_V), blk(GLA_V), blk(GLA_QK),
                  pl.BlockSpec((1, GLA_DV), lambda b, c: (0, 0))],
        out_specs=blk(GLA_V),
        out_shape=jax.ShapeDtypeStruct((batch * seq, GLA_V), BF16),
        scratch_shapes=[pltpu.VMEM((GLA_HEADS, GLA_DK, GLA_DV), F32)],
        compiler_params=pltpu.CompilerParams(
            dimension_semantics=("parallel", "arbitrary"), vmem_limit_bytes=VMEM_LIMIT),
        name="gla_chunk",
    )(q, k, v, r, bcum, norm_g)


def _swa_qkv_kernel(h_ref, wq_ref, wk_ref, wv_ref, bq_ref, bk_ref, bv_ref, q_ref, k_ref, v_ref):
    hb = h_ref[...].astype(BF16)
    q_ref[...] = ((_dot(hb, wq_ref[...]) + bq_ref[...]) * (SWA_HEAD_DIM ** -0.5)).astype(BF16)
    k_ref[...] = (_dot(hb, wk_ref[...]) + bk_ref[...]).astype(BF16)
    v_ref[...] = (_dot(hb, wv_ref[...]) + bv_ref[...]).astype(BF16)


def _swa_qkv(h2, wq, wk, wv, bq, bk, bv):
    t = h2.shape[0]
    row = lambda w: pl.BlockSpec((ROW_TILE, w), lambda i: (i, 0))
    return pl.pallas_call(
        _swa_qkv_kernel,
        grid=(t // ROW_TILE,),
        in_specs=[row(D_MODEL), _resident(wq.shape), _resident(wk.shape), _resident(wv.shape),
                  _resident(bq.shape), _resident(bk.shape), _resident(bv.shape)],
        out_specs=[row(D_MODEL), row(SWA_KV), row(SWA_KV)],
        out_shape=[jax.ShapeDtypeStruct((t, D_MODEL), BF16),
                   jax.ShapeDtypeStruct((t, SWA_KV), BF16),
                   jax.ShapeDtypeStruct((t, SWA_KV), BF16)],
        compiler_params=pltpu.CompilerParams(
            dimension_semantics=("parallel",), vmem_limit_bytes=VMEM_LIMIT),
        name="swa_qkv",
    )(h2, wq, wk, wv, bq, bk, bv)


def _swa_attn_kernel(sink_ref, q_ref, kp_ref, kc_ref, vp_ref, vc_ref, o_ref):
    has_prev = (pl.program_id(1) > 0).astype(BF16)
    k_all = jnp.concatenate([kp_ref[...] * has_prev, kc_ref[...]], axis=0)
    v_all = jnp.concatenate([vp_ref[...] * has_prev, vc_ref[...]], axis=0)
    qi = lax.broadcasted_iota(jnp.int32, (SWA_BLOCK, SWA_BLOCK), 0)
    jj = lax.broadcasted_iota(jnp.int32, (SWA_BLOCK, SWA_BLOCK), 1)
    from_prev = jj > qi
    even_lanes = jj < SWA_HEAD_DIM
    one = jnp.ones((2 * SWA_BLOCK, LANES), BF16)
    pairs_per_kv = SWA_GROUP // 2
    for blk in range(SWA_Q_TILE // SWA_BLOCK):
        band = slice(blk * SWA_BLOCK, (blk + 2) * SWA_BLOCK)
        rows = slice(blk * SWA_BLOCK, (blk + 1) * SWA_BLOCK)
        k2, v2 = [], []
        for kv in range(SWA_KV_HEADS):
            cs = slice(kv * SWA_HEAD_DIM, (kv + 1) * SWA_HEAD_DIM)
            kh = k_all[band, cs]
            vh = v_all[band, cs]
            k2.append(jnp.concatenate([kh, kh], axis=1))
            v2.append(jnp.concatenate([vh, vh, one], axis=1))
        scores, maxes = [], []
        for kv in range(SWA_KV_HEADS):
            q_heads = []
            for j in range(pairs_per_kv):
                hp = kv * pairs_per_kv + j
                q_pair = q_ref[rows, hp * LANES:(hp + 1) * LANES]
                q_heads += [jnp.where(even_lanes, q_pair, jnp.zeros_like(q_pair)),
                            jnp.where(even_lanes, jnp.zeros_like(q_pair), q_pair)]
            s_grp = _dot_nt(jnp.concatenate(q_heads, axis=0), k2[kv])
            for g in range(SWA_GROUP):
                s2 = s_grp[g * SWA_BLOCK:(g + 1) * SWA_BLOCK]
                s = jnp.where(from_prev, s2[:, :SWA_BLOCK], s2[:, SWA_BLOCK:])
                scores.append(s)
                maxes.append(jnp.maximum(jnp.max(s, axis=-1, keepdims=True),
                                         sink_ref[kv * SWA_GROUP + g]))
        probs = []
        for h in range(SWA_Q_HEADS):
            p = jnp.exp(scores[h] - maxes[h])
            probs.append(jnp.concatenate([jnp.where(from_prev, p, 0.0),
                                          jnp.where(from_prev, 0.0, p)], axis=1).astype(BF16))
        pvs = []
        for kv in range(SWA_KV_HEADS):
            p_grp = jnp.concatenate(probs[kv * SWA_GROUP:(kv + 1) * SWA_GROUP], axis=0)
            pv_grp = _dot(p_grp, v2[kv])
            pvs += [pv_grp[g * SWA_BLOCK:(g + 1) * SWA_BLOCK] for g in range(SWA_GROUP)]
        for hp in range(SWA_Q_HEADS // 2):
            pv_e, pv_o = pvs[2 * hp], pvs[2 * hp + 1]
            num = jnp.where(even_lanes, pv_e[:, :LANES], pv_o[:, :LANES])
            den = jnp.where(even_lanes,
                            pv_e[:, LANES:] + jnp.exp(sink_ref[2 * hp] - maxes[2 * hp]),
                            pv_o[:, LANES:] + jnp.exp(sink_ref[2 * hp + 1] - maxes[2 * hp + 1]))
            o_ref[rows, hp * LANES:(hp + 1) * LANES] = (num / den).astype(BF16)


def _swa_attn(sinks, q, k, v, batch, seq):
    nt = seq // SWA_Q_TILE
    per = SWA_Q_TILE // SWA_BLOCK
    cur = lambda w: pl.BlockSpec((SWA_Q_TILE, w), lambda b, n, *_: (b * nt + n, 0))
    prev = lambda w: pl.BlockSpec(
        (SWA_BLOCK, w), lambda b, n, *_: (b * nt * per + jnp.maximum(n * per - 1, 0), 0))
    return pl.pallas_call(
        _swa_attn_kernel,
        grid_spec=pltpu.PrefetchScalarGridSpec(
            num_scalar_prefetch=1,
            grid=(batch, nt),
            in_specs=[cur(D_MODEL), prev(SWA_KV), cur(SWA_KV), prev(SWA_KV), cur(SWA_KV)],
            out_specs=cur(D_MODEL)),
        out_shape=jax.ShapeDtypeStruct((batch * seq, D_MODEL), BF16),
        compiler_params=pltpu.CompilerParams(
            dimension_semantics=("parallel", "parallel"), vmem_limit_bytes=VMEM_LIMIT),
        name="swa_attn",
    )(sinks, q, k, k, v, v)


def _post_mixer_kernel(mix_ref, h_ref, p_ref, wo_ref, bo_ref, g1_ref, b1_ref, wup_ref, wdn_ref,
                       g2_ref, b2_ref, wg_ref, bg_ref, wp_ref, out_ref):
    y = DEEPNORM_ALPHA * h_ref[...] + _dot(mix_ref[...], wo_ref[...]) + bo_ref[...]
    h1 = _layer_norm(y, g1_ref[...], b1_ref[...])
    h1b = h1.astype(BF16)
    acc = DEEPNORM_ALPHA * h1
    for c in range(D_FF // FF_TILE):
        u = jnp.maximum(_dot(h1b, wup_ref[:, c * FF_TILE:(c + 1) * FF_TILE]), 0.0)
        acc = acc + _dot((u * u).astype(BF16), wdn_ref[c * FF_TILE:(c + 1) * FF_TILE, :])
    h2 = _layer_norm(acc, g2_ref[...], b2_ref[...])
    z = _dot(h2.astype(BF16), wg_ref[...]) + bg_ref[...]
    gate = 1.0 / (1.0 + jnp.exp(-z))
    out_ref[...] = h2 + gate * _dot(p_ref[...].astype(BF16), wp_ref[...])


def _post_mixer(mix, h2d, p3, layer, wo, bo, g1, b1, wup, wdn, g2, b2, wg, bg, wp):
    t = h2d.shape[0]
    row = lambda w: pl.BlockSpec((ROW_TILE, w), lambda i: (i, 0))
    weights = (wo, bo, g1, b1, wup, wdn, g2, b2, wg, bg, wp)
    return pl.pallas_call(
        _post_mixer_kernel,
        grid=(t // ROW_TILE,),
        in_specs=[row(D_MODEL), row(D_MODEL),
                  pl.BlockSpec((None, ROW_TILE, PLE_DIM), lambda i: (layer, i, 0))]
                 + [_resident(w.shape) for w in weights],
        out_specs=row(D_MODEL),
        out_shape=jax.ShapeDtypeStruct((t, D_MODEL), F32),
        compiler_params=pltpu.CompilerParams(
            dimension_semantics=("parallel",), vmem_limit_bytes=VMEM_LIMIT),
        name=f"post_mixer_{layer}",
    )(mix, h2d, p3, *weights)


def kernel(x, p, gla_w_in, gla_w_gk_up, gla_b_gk, gla_norm_g, gla_w_out, swa_w_qkv, swa_b_qkv,
           swa_sinks, swa_w_out, swa_b_out, mlp_w_up, mlp_w_down, ln1_g, ln1_b, ln2_g, ln2_b,
           ple_w_proj, ple_w_gate, ple_b_gate):
    batch, seq, d = x.shape
    t = batch * seq
    x2 = x.reshape(t, d)
    p3 = p.reshape(DEPTH, t, PLE_DIM)
    vec = lambda a: a.reshape(1, -1).astype(F32)

    def post(mix, h, layer, wo, bo):
        return _post_mixer(
            mix, h, p3, layer, wo.astype(BF16), vec(bo), vec(ln1_g[layer]), vec(ln1_b[layer]),
            mlp_w_up[layer].astype(BF16), mlp_w_down[layer].astype(BF16),
            vec(ln2_g[layer]), vec(ln2_b[layer]), ple_w_gate[layer].astype(BF16),
            vec(ple_b_gate[layer]), ple_w_proj[layer].astype(BF16))

    w_in = gla_w_in[0].astype(BF16)
    c0, c1, c2, c3 = GLA_QK, 2 * GLA_QK, 2 * GLA_QK + GLA_V, 2 * GLA_QK + 2 * GLA_V
    wgl = jnp.pad(w_in[:, c3:], ((0, 0), (0, LANES - GLA_LOWRANK)))
    wup = jnp.pad(gla_w_gk_up[0].astype(BF16), ((0, LANES - GLA_LOWRANK), (0, 0)))
    q, k, v, r, bcum = _gla_inproj(x2, w_in[:, :c0], w_in[:, c0:c1], w_in[:, c1:c2],
                                   w_in[:, c2:c3], wgl, wup, vec(gla_b_gk[0]))
    mix = _gla_chunk(q, k, v, r, bcum, vec(gla_norm_g[0]), batch, seq)
    h = post(mix, x2, 0, gla_w_out[0], jnp.zeros((d,), F32))

    w_qkv = swa_w_qkv[0].astype(BF16)
    b_qkv = swa_b_qkv[0]
    q, k, v = _swa_qkv(h, w_qkv[:, :D_MODEL], w_qkv[:, D_MODEL:D_MODEL + SWA_KV],
                       w_qkv[:, D_MODEL + SWA_KV:], vec(b_qkv[:D_MODEL]),
                       vec(b_qkv[D_MODEL:D_MODEL + SWA_KV]), vec(b_qkv[D_MODEL + SWA_KV:]))
    mix = _swa_attn(swa_sinks[0].astype(F32), q, k, v, batch, seq)
    h = post(mix, h, 1, swa_w_out[0], swa_b_out[0])
    return h.reshape(batch, seq, d)
```

```python
import functools

import jax
import jax.numpy as jnp
from jax import lax
from jax.experimental import pallas as pl
from jax.experimental.pallas import tpu as pltpu

D_MODEL = 1024
DEPTH = 2
PLE_DIM = 256

GLA_HEADS = 4
GLA_DK = 128
GLA_DV = 256
GLA_LOWRANK = 16
GLA_TAU = 16.0
GLA_QK = GLA_HEADS * GLA_DK
GLA_V = GLA_HEADS * GLA_DV

SWA_HEAD_DIM = 64
SWA_Q_HEADS = 16
SWA_KV_HEADS = 4
SWA_GROUP = SWA_Q_HEADS // SWA_KV_HEADS
SWA_BLOCK = 128
SWA_Q_TILE = 256
SWA_KV = SWA_KV_HEADS * SWA_HEAD_DIM

D_FF = 4 * D_MODEL
DEEPNORM_ALPHA = (2.0 * DEPTH) ** 0.25
LN_EPS = 1e-5
RMS_EPS = 1e-5

LANES = 128
GLA_CHUNK = 128
ROW_TILE = 512
FF_TILE = 1024
VMEM_LIMIT = 56 * 1024 * 1024

BF16 = jnp.bfloat16
F32 = jnp.float32


def _dot(a, b):
    return jnp.dot(a, b, preferred_element_type=F32)


def _dot_nt(a, b):
    return lax.dot_general(a, b, (((1,), (1,)), ((), ())), preferred_element_type=F32)


def _resident(shape):
    nd = len(shape)
    return pl.BlockSpec(shape, lambda *_: (0,) * nd, pipeline_mode=pl.Buffered(1))


def _layer_norm(y, g, b):
    mu = jnp.mean(y, axis=-1, keepdims=True)
    yc = y - mu
    var = jnp.mean(yc * yc, axis=-1, keepdims=True)
    return yc * lax.rsqrt(var + LN_EPS) * g + b


def _gla_inproj_kernel(x_ref, wq_ref, wk_ref, wv_ref, wr_ref, wgl_ref, wup_ref, bgk_ref,
                       q_ref, k_ref, v_ref, r_ref, bcum_ref):
    xb = x_ref[...].astype(BF16)
    gl = _dot(xb, wgl_ref[...]).astype(BF16)
    q_ref[...] = (_dot(xb, wq_ref[...]) * (GLA_DK ** -0.5)).astype(BF16)
    z = _dot(gl, wup_ref[...]) + bgk_ref[...]
    k_ref[...] = _dot(xb, wk_ref[...]).astype(BF16)
    log_a = (jnp.minimum(z, 0.0) - jnp.log1p(jnp.exp(-jnp.abs(z)))) * (1.0 / GLA_TAU)
    hi = log_a.astype(BF16)
    rem = log_a - hi.astype(F32)
    mid = rem.astype(BF16)
    lo = (rem - mid.astype(F32)).astype(BF16)
    v_ref[...] = _dot(xb, wv_ref[...]).astype(BF16)
    row = lax.broadcasted_iota(jnp.int32, (GLA_CHUNK, GLA_CHUNK), 0)
    col = lax.broadcasted_iota(jnp.int32, (GLA_CHUNK, GLA_CHUNK), 1)
    tril = (col <= row).astype(BF16)
    for c in range(ROW_TILE // GLA_CHUNK):
        rows = slice(c * GLA_CHUNK, (c + 1) * GLA_CHUNK)
        bcum_ref[rows, :] = _dot(tril, hi[rows]) + _dot(tril, mid[rows]) + _dot(tril, lo[rows])
    r_ref[...] = _dot(xb, wr_ref[...]).astype(BF16)


def _gla_inproj(x2, wq, wk, wv, wr, wgl, wup, bgk):
    t = x2.shape[0]
    row = lambda w: pl.BlockSpec((ROW_TILE, w), lambda i: (i, 0))
    return pl.pallas_call(
        _gla_inproj_kernel,
        grid=(t // ROW_TILE,),
        in_specs=[row(D_MODEL), _resident(wq.shape), _resident(wk.shape), _resident(wv.shape),
                  _resident(wr.shape), _resident(wgl.shape), _resident(wup.shape),
                  _resident(bgk.shape)],
        out_specs=[row(GLA_QK), row(GLA_QK), row(GLA_V), row(GLA_V), row(GLA_QK)],
        out_shape=[jax.ShapeDtypeStruct((t, GLA_QK), BF16),
                   jax.ShapeDtypeStruct((t, GLA_QK), BF16),
                   jax.ShapeDtypeStruct((t, GLA_V), BF16),
                   jax.ShapeDtypeStruct((t, GLA_V), BF16),
                   jax.ShapeDtypeStruct((t, GLA_QK), F32)],
        compiler_params=pltpu.CompilerParams(
            dimension_semantics=("parallel",), vmem_limit_bytes=VMEM_LIMIT),
        name="gla_inproj",
    )(x2, wq, wk, wv, wr, wgl, wup, bgk)


def _gla_chunk_kernel(q_ref, k_ref, v_ref, r_ref, b_ref, ng_ref, o_ref, state_ref):
    @pl.when(pl.program_id(1) == 0)
    def _():
        state_ref[...] = jnp.zeros_like(state_ref)

    row = lax.broadcasted_iota(jnp.int32, (GLA_CHUNK, GLA_CHUNK), 0)
    col = lax.broadcasted_iota(jnp.int32, (GLA_CHUNK, GLA_CHUNK), 1)
    causal = col <= row
    ng = ng_ref[...]
    heads = range(GLA_HEADS)
    ks = [slice(h * GLA_DK, (h + 1) * GLA_DK) for h in heads]
    vs = [slice(h * GLA_DV, (h + 1) * GLA_DV) for h in heads]
    state = [state_ref[h] for h in heads]
    for c in range(ROW_TILE // GLA_CHUNK):
        rows = slice(c * GLA_CHUNK, (c + 1) * GLA_CHUNK)
        qd, kd, kl, dec = [], [], [], []
        for h in heads:
            b = b_ref[rows, ks[h]]
            d_last = jnp.exp(b[GLA_CHUNK - 1:GLA_CHUNK, :])
            k_dec = k_ref[rows, ks[h]].astype(F32) * jnp.exp(-b)
            qd.append((q_ref[rows, ks[h]].astype(F32) * jnp.exp(b)).astype(BF16))
            kd.append(k_dec.astype(BF16))
            kl.append((k_dec * d_last).astype(BF16))
            dec.append(d_last)
        att = [jnp.where(causal, _dot_nt(qd[h], kd[h]), 0.0).astype(BF16) for h in heads]
        o = [_dot(att[h], v_ref[rows, vs[h]]) + _dot_nt(qd[h], state[h].astype(BF16))
             for h in heads]
        upd = [lax.dot_general(v_ref[rows, vs[h]], kl[h], (((0,), (0,)), ((), ())),
                               preferred_element_type=F32) for h in heads]
        state = [state[h] * dec[h] + upd[h] for h in heads]
        for h in heads:
            on = o[h] * lax.rsqrt(jnp.mean(o[h] * o[h], axis=-1, keepdims=True) + RMS_EPS) * ng
            r = r_ref[rows, vs[h]].astype(F32)
            o_ref[rows, vs[h]] = (on * (r / (1.0 + jnp.exp(-r)))).astype(BF16)
    for h in heads:
        state_ref[h] = state[h]


def _gla_chunk(q, k, v, r, bcum, norm_g, batch, seq):
    nc = seq // ROW_TILE
    blk = lambda w: pl.BlockSpec((ROW_TILE, w), lambda b, c: (b * nc + c, 0))
    return pl.pallas_call(
        _gla_chunk_kernel,
        grid=(batch, nc),
        in_specs=[blk(GLA_QK), blk(GLA_QK), blk(GLA_V), blk(GLA_V), blk(GLA_QK),
                  pl.BlockSpec((1, GLA_DV), lambda b, c: (0, 0))],
        out_specs=blk(GLA_V),
        out_shape=jax.ShapeDtypeStruct((batch * seq, GLA_V), BF16),
        scratch_shapes=[pltpu.VMEM((GLA_HEADS, GLA_DV, GLA_DK), F32)],
        compiler_params=pltpu.CompilerParams(
            dimension_semantics=("parallel", "arbitrary"), vmem_limit_bytes=VMEM_LIMIT),
        name="gla_chunk",
    )(q, k, v, r, bcum, norm_g)


def _swa_qkv_kernel(h_ref, wq_ref, wk_ref, wv_ref, bq_ref, bk_ref, bv_ref, q_ref, k_ref, v_ref):
    hb = h_ref[...].astype(BF16)
    q_ref[...] = ((_dot(hb, wq_ref[...]) + bq_ref[...]) * (SWA_HEAD_DIM ** -0.5)).astype(BF16)
    k_ref[...] = (_dot(hb, wk_ref[...]) + bk_ref[...]).astype(BF16)
    v_ref[...] = (_dot(hb, wv_ref[...]) + bv_ref[...]).astype(BF16)


def _swa_qkv(h2, wq, wk, wv, bq, bk, bv):
    t = h2.shape[0]
    row = lambda w: pl.BlockSpec((ROW_TILE, w), lambda i: (i, 0))
    return pl.pallas_call(
        _swa_qkv_kernel,
        grid=(t // ROW_TILE,),
        in_specs=[row(D_MODEL), _resident(wq.shape), _resident(wk.shape), _resident(wv.shape),
                  _resident(bq.shape), _resident(bk.shape), _resident(bv.shape)],
        out_specs=[row(D_MODEL), row(SWA_KV), row(SWA_KV)],
        out_shape=[jax.ShapeDtypeStruct((t, D_MODEL), BF16),
                   jax.ShapeDtypeStruct((t, SWA_KV), BF16),
                   jax.ShapeDtypeStruct((t, SWA_KV), BF16)],
        compiler_params=pltpu.CompilerParams(
            dimension_semantics=("parallel",), vmem_limit_bytes=VMEM_LIMIT),
        name="swa_qkv",
    )(h2, wq, wk, wv, bq, bk, bv)


def _swa_attn_kernel(sink_ref, q_ref, kp_ref, kc_ref, vp_ref, vc_ref, o_ref):
    has_prev = (pl.program_id(1) > 0).astype(BF16)
    k_all = jnp.concatenate([kp_ref[...] * has_prev, kc_ref[...]], axis=0)
    v_all = jnp.concatenate([vp_ref[...] * has_prev, vc_ref[...]], axis=0)
    qi = lax.broadcasted_iota(jnp.int32, (SWA_BLOCK, SWA_BLOCK), 0)
    jj = lax.broadcasted_iota(jnp.int32, (SWA_BLOCK, SWA_BLOCK), 1)
    from_prev = jj > qi
    even_lanes = jj < SWA_HEAD_DIM
    one = jnp.ones((2 * SWA_BLOCK, LANES), BF16)
    pairs_per_kv = SWA_GROUP // 2
    for blk in range(SWA_Q_TILE // SWA_BLOCK):
        band = slice(blk * SWA_BLOCK, (blk + 2) * SWA_BLOCK)
        rows = slice(blk * SWA_BLOCK, (blk + 1) * SWA_BLOCK)
        k2, v2 = [], []
        for kv in range(SWA_KV_HEADS):
            cs = slice(kv * SWA_HEAD_DIM, (kv + 1) * SWA_HEAD_DIM)
            kh = k_all[band, cs]
            vh = v_all[band, cs]
            k2.append(jnp.concatenate([kh, kh], axis=1))
            v2.append(jnp.concatenate([vh, vh, one], axis=1))
        scores, maxes = [], []
        for kv in range(SWA_KV_HEADS):
            q_heads = []
            for j in range(pairs_per_kv):
                hp = kv * pairs_per_kv + j
                q_pair = q_ref[rows, hp * LANES:(hp + 1) * LANES]
                q_heads += [jnp.where(even_lanes, q_pair, jnp.zeros_like(q_pair)),
                            jnp.where(even_lanes, jnp.zeros_like(q_pair), q_pair)]
            s_grp = _dot_nt(jnp.concatenate(q_heads, axis=0), k2[kv])
            for g in range(SWA_GROUP):
                s2 = s_grp[g * SWA_BLOCK:(g + 1) * SWA_BLOCK]
                s = jnp.where(from_prev, s2[:, :SWA_BLOCK], s2[:, SWA_BLOCK:])
                scores.append(s)
                maxes.append(jnp.maximum(jnp.max(s, axis=-1, keepdims=True),
                                         sink_ref[kv * SWA_GROUP + g]))
        probs = []
        for h in range(SWA_Q_HEADS):
            p = jnp.exp(scores[h] - maxes[h])
            probs.append(jnp.concatenate([jnp.where(from_prev, p, 0.0),
                                          jnp.where(from_prev, 0.0, p)], axis=1).astype(BF16))
        pvs = []
        for kv in range(SWA_KV_HEADS):
            p_grp = jnp.concatenate(probs[kv * SWA_GROUP:(kv + 1) * SWA_GROUP], axis=0)
            pv_grp = _dot(p_grp, v2[kv])
            pvs += [pv_grp[g * SWA_BLOCK:(g + 1) * SWA_BLOCK] for g in range(SWA_GROUP)]
        for hp in range(SWA_Q_HEADS // 2):
            pv_e, pv_o = pvs[2 * hp], pvs[2 * hp + 1]
            num = jnp.where(even_lanes, pv_e[:, :LANES], pv_o[:, :LANES])
            den = jnp.where(even_lanes,
                            pv_e[:, LANES:] + jnp.exp(sink_ref[2 * hp] - maxes[2 * hp]),
                            pv_o[:, LANES:] + jnp.exp(sink_ref[2 * hp + 1] - maxes[2 * hp + 1]))
            o_ref[rows, hp * LANES:(hp + 1) * LANES] = (num / den).astype(BF16)


def _swa_attn(sinks, q, k, v, batch, seq):
    nt = seq // SWA_Q_TILE
    per = SWA_Q_TILE // SWA_BLOCK
    cur = lambda w: pl.BlockSpec((SWA_Q_TILE, w), lambda b, n, *_: (b * nt + n, 0))
    prev = lambda w: pl.BlockSpec(
        (SWA_BLOCK, w), lambda b, n, *_: (b * nt * per + jnp.maximum(n * per - 1, 0), 0))
    return pl.pallas_call(
        _swa_attn_kernel,
        grid_spec=pltpu.PrefetchScalarGridSpec(
            num_scalar_prefetch=1,
            grid=(batch, nt),
            in_specs=[cur(D_MODEL), prev(SWA_KV), cur(SWA_KV), prev(SWA_KV), cur(SWA_KV)],
            out_specs=cur(D_MODEL)),
        out_shape=jax.ShapeDtypeStruct((batch * seq, D_MODEL), BF16),
        compiler_params=pltpu.CompilerParams(
            dimension_semantics=("parallel", "parallel"), vmem_limit_bytes=VMEM_LIMIT),
        name="swa_attn",
    )(sinks, q, k, k, v, v)


def _post_mixer_kernel(mix_ref, h_ref, p_ref, wo_ref, bo_ref, g1_ref, b1_ref, wup_ref, wdn_ref,
                       g2_ref, b2_ref, wg_ref, bg_ref, wp_ref, out_ref):
    half = ROW_TILE // 2
    rows = [slice(0, half), slice(half, ROW_TILE)]

    def out_proj(r):
        return DEEPNORM_ALPHA * h_ref[r, :] + _dot(mix_ref[r, :], wo_ref[...]) + bo_ref[...]

    def mlp(h1b, acc, chunks):
        for c in chunks:
            u = jnp.maximum(_dot(h1b, wup_ref[:, c * FF_TILE:(c + 1) * FF_TILE]), 0.0)
            acc = acc + _dot((u * u).astype(BF16), wdn_ref[c * FF_TILE:(c + 1) * FF_TILE, :])
        return acc

    def ple(r, h2):
        z = _dot(h2.astype(BF16), wg_ref[...]) + bg_ref[...]
        gate = 1.0 / (1.0 + jnp.exp(-z))
        out_ref[r, :] = h2 + gate * _dot(p_ref[r, :].astype(BF16), wp_ref[...])

    n_ff = D_FF // FF_TILE
    y_a = out_proj(rows[0])
    y_b = out_proj(rows[1])
    h1_a = _layer_norm(y_a, g1_ref[...], b1_ref[...])
    h1b_a = h1_a.astype(BF16)
    acc_a = mlp(h1b_a, DEEPNORM_ALPHA * h1_a, range(0, 1))
    h1_b = _layer_norm(y_b, g1_ref[...], b1_ref[...])
    h1b_b = h1_b.astype(BF16)
    acc_a = mlp(h1b_a, acc_a, range(1, n_ff))
    acc_b = mlp(h1b_b, DEEPNORM_ALPHA * h1_b, range(0, 1))
    h2_a = _layer_norm(acc_a, g2_ref[...], b2_ref[...])
    acc_b = mlp(h1b_b, acc_b, range(1, n_ff))
    h2_b = _layer_norm(acc_b, g2_ref[...], b2_ref[...])
    ple(rows[0], h2_a)
    ple(rows[1], h2_b)


def _post_mixer(mix, h2d, p3, layer, wo, bo, g1, b1, wup, wdn, g2, b2, wg, bg, wp):
    t = h2d.shape[0]
    row = lambda w: pl.BlockSpec((ROW_TILE, w), lambda i: (i, 0))
    weights = (wo, bo, g1, b1, wup, wdn, g2, b2, wg, bg, wp)
    return pl.pallas_call(
        _post_mixer_kernel,
        grid=(t // ROW_TILE,),
        in_specs=[row(D_MODEL), row(D_MODEL),
                  pl.BlockSpec((None, ROW_TILE, PLE_DIM), lambda i: (layer, i, 0))]
                 + [_resident(w.shape) for w in weights],
        out_specs=row(D_MODEL),
        out_shape=jax.ShapeDtypeStruct((t, D_MODEL), F32),
        compiler_params=pltpu.CompilerParams(
            dimension_semantics=("parallel",), vmem_limit_bytes=VMEM_LIMIT),
        name=f"post_mixer_{layer}",
    )(mix, h2d, p3, *weights)


def kernel(x, p, gla_w_in, gla_w_gk_up, gla_b_gk, gla_norm_g, gla_w_out, swa_w_qkv, swa_b_qkv,
           swa_sinks, swa_w_out, swa_b_out, mlp_w_up, mlp_w_down, ln1_g, ln1_b, ln2_g, ln2_b,
           ple_w_proj, ple_w_gate, ple_b_gate):
    batch, seq, d = x.shape
    t = batch * seq
    x2 = x.reshape(t, d)
    p3 = p.reshape(DEPTH, t, PLE_DIM)
    vec = lambda a: a.reshape(1, -1).astype(F32)

    def post(mix, h, layer, wo, bo):
        return _post_mixer(
            mix, h, p3, layer, wo.astype(BF16), vec(bo), vec(ln1_g[layer]), vec(ln1_b[layer]),
            mlp_w_up[layer].astype(BF16), mlp_w_down[layer].astype(BF16),
            vec(ln2_g[layer]), vec(ln2_b[layer]), ple_w_gate[layer].astype(BF16),
            vec(ple_b_gate[layer]), ple_w_proj[layer].astype(BF16))

    w_in = gla_w_in[0].astype(BF16)
    c0, c1, c2, c3 = GLA_QK, 2 * GLA_QK, 2 * GLA_QK + GLA_V, 2 * GLA_QK + 2 * GLA_V
    wgl = jnp.pad(w_in[:, c3:], ((0, 0), (0, LANES - GLA_LOWRANK)))
    wup = jnp.pad(gla_w_gk_up[0].astype(BF16), ((0, LANES - GLA_LOWRANK), (0, 0)))
    q, k, v, r, bcum = _gla_inproj(x2, w_in[:, :c0], w_in[:, c0:c1], w_in[:, c1:c2],
                                   w_in[:, c2:c3], wgl, wup, vec(gla_b_gk[0]))
    mix = _gla_chunk(q, k, v, r, bcum, vec(gla_norm_g[0]), batch, seq)
    h = post(mix, x2, 0, gla_w_out[0], jnp.zeros((d,), F32))

    w_qkv = swa_w_qkv[0].astype(BF16)
    b_qkv = swa_b_qkv[0]
    q, k, v = _swa_qkv(h, w_qkv[:, :D_MODEL], w_qkv[:, D_MODEL:D_MODEL + SWA_KV],
                       w_qkv[:, D_MODEL + SWA_KV:], vec(b_qkv[:D_MODEL]),
                       vec(b_qkv[D_MODEL:D_MODEL + SWA_KV]), vec(b_qkv[D_MODEL + SWA_KV:]))
    mix = _swa_attn(swa_sinks[0].astype(F32), q, k, v, batch, seq)
    h = post(mix, h, 1, swa_w_out[0], swa_b_out[0])
    return h.reshape(batch, seq, d)
```

```python
import functools

import jax
import jax.numpy as jnp
from jax import lax
from jax.experimental import pallas as pl
from jax.experimental.pallas import tpu as pltpu

D_MODEL = 1024
DEPTH = 2
PLE_DIM = 256

GLA_HEADS = 4
GLA_DK = 128
GLA_DV = 256
GLA_LOWRANK = 16
GLA_TAU = 16.0
GLA_QK = GLA_HEADS * GLA_DK
GLA_V = GLA_HEADS * GLA_DV

SWA_HEAD_DIM = 64
SWA_Q_HEADS = 16
SWA_KV_HEADS = 4
SWA_GROUP = SWA_Q_HEADS // SWA_KV_HEADS
SWA_BLOCK = 128
SWA_Q_TILE = 256
SWA_KV = SWA_KV_HEADS * SWA_HEAD_DIM

D_FF = 4 * D_MODEL
DEEPNORM_ALPHA = (2.0 * DEPTH) ** 0.25
LN_EPS = 1e-5
RMS_EPS = 1e-5

LANES = 128
GLA_CHUNK = 128
ROW_TILE = 512
POST_TILE = 512
FF_TILE = 1024
VMEM_LIMIT = 60 * 1024 * 1024

BF16 = jnp.bfloat16
F32 = jnp.float32


def _dot(a, b):
    return jnp.dot(a, b, preferred_element_type=F32)


def _dot_nt(a, b):
    return lax.dot_general(a, b, (((1,), (1,)), ((), ())), preferred_element_type=F32)


def _resident(shape):
    nd = len(shape)
    return pl.BlockSpec(shape, lambda *_: (0,) * nd, pipeline_mode=pl.Buffered(1))


def _layer_norm(y, g, b):
    mu = jnp.mean(y, axis=-1, keepdims=True)
    yc = y - mu
    var = jnp.mean(yc * yc, axis=-1, keepdims=True)
    return yc * lax.rsqrt(var + LN_EPS) * g + b


def _gla_inproj_kernel(x_ref, w_ref, wup_ref, bgk_ref, ng_ref,
                       q_ref, k_ref, v_ref, g_ref, bcum_ref):
    c_k, c_v, c_r, c_gl = GLA_QK, 2 * GLA_QK, 2 * GLA_QK + GLA_V, 2 * GLA_QK + 2 * GLA_V
    xb = x_ref[...].astype(BF16)
    gl = _dot(xb, w_ref[:, c_gl:]).astype(BF16)
    r = _dot(xb, w_ref[:, c_r:c_gl])
    z = _dot(gl, wup_ref[...]) + bgk_ref[...]
    q_ref[...] = (_dot(xb, w_ref[:, :c_k]) * (GLA_DK ** -0.5)).astype(BF16)
    g_ref[...] = (r / (1.0 + jnp.exp(-r)) * ng_ref[...]).astype(BF16)
    k_ref[...] = _dot(xb, w_ref[:, c_k:c_v]).astype(BF16)
    log_a = (jnp.minimum(z, 0.0) - jnp.log1p(jnp.exp(-jnp.abs(z)))) * (1.0 / GLA_TAU)
    hi = log_a.astype(BF16)
    rem = log_a - hi.astype(F32)
    mid = rem.astype(BF16)
    lo = (rem - mid.astype(F32)).astype(BF16)
    v_ref[...] = _dot(xb, w_ref[:, c_v:c_r]).astype(BF16)
    row = lax.broadcasted_iota(jnp.int32, (GLA_CHUNK, GLA_CHUNK), 0)
    col = lax.broadcasted_iota(jnp.int32, (GLA_CHUNK, GLA_CHUNK), 1)
    tril = (col <= row).astype(BF16)
    for c in range(ROW_TILE // GLA_CHUNK):
        rows = slice(c * GLA_CHUNK, (c + 1) * GLA_CHUNK)
        bcum_ref[rows, :] = _dot(tril, hi[rows]) + _dot(tril, mid[rows]) + _dot(tril, lo[rows])


def _gla_inproj(x2, w_in, wup, bgk, ng):
    t = x2.shape[0]
    row = lambda w: pl.BlockSpec((ROW_TILE, w), lambda i: (i, 0))
    return pl.pallas_call(
        _gla_inproj_kernel,
        grid=(t // ROW_TILE,),
        in_specs=[row(D_MODEL), _resident(w_in.shape), _resident(wup.shape),
                  _resident(bgk.shape), _resident(ng.shape)],
        out_specs=[row(GLA_QK), row(GLA_QK), row(GLA_V), row(GLA_V), row(GLA_QK)],
        out_shape=[jax.ShapeDtypeStruct((t, GLA_QK), BF16),
                   jax.ShapeDtypeStruct((t, GLA_QK), BF16),
                   jax.ShapeDtypeStruct((t, GLA_V), BF16),
                   jax.ShapeDtypeStruct((t, GLA_V), BF16),
                   jax.ShapeDtypeStruct((t, GLA_QK), F32)],
        compiler_params=pltpu.CompilerParams(
            dimension_semantics=("parallel",), vmem_limit_bytes=VMEM_LIMIT),
        name="gla_inproj",
    )(x2, w_in, wup, bgk, ng)


def _gla_chunk_kernel(q_ref, k_ref, v_ref, g_ref, b_ref, o_ref, state_ref):
    @pl.when(pl.program_id(1) == 0)
    def _():
        state_ref[...] = jnp.zeros_like(state_ref)

    row = lax.broadcasted_iota(jnp.int32, (GLA_CHUNK, GLA_CHUNK), 0)
    col = lax.broadcasted_iota(jnp.int32, (GLA_CHUNK, GLA_CHUNK), 1)
    causal = col <= row
    heads = range(GLA_HEADS)
    ks = [slice(h * GLA_DK, (h + 1) * GLA_DK) for h in heads]
    vs = [slice(h * GLA_DV, (h + 1) * GLA_DV) for h in heads]
    state = [state_ref[h] for h in heads]
    for c in range(ROW_TILE // GLA_CHUNK):
        rows = slice(c * GLA_CHUNK, (c + 1) * GLA_CHUNK)
        qd, kd, kl, dec = [], [], [], []
        for h in heads:
            b = b_ref[rows, ks[h]]
            d_last = jnp.exp(b[GLA_CHUNK - 1:GLA_CHUNK, :])
            k_dec = k_ref[rows, ks[h]] * jnp.exp(-b).astype(BF16)
            qd.append(q_ref[rows, ks[h]] * jnp.exp(b).astype(BF16))
            kd.append(k_dec)
            kl.append(k_dec * d_last.astype(BF16))
            dec.append(jnp.broadcast_to(d_last, (GLA_CHUNK, GLA_DK)).T)
        att = [jnp.where(causal, _dot_nt(qd[h], kd[h]), 0.0).astype(BF16) for h in heads]
        o = [_dot(jnp.concatenate([att[h], qd[h]], axis=1),
                  jnp.concatenate([v_ref[rows, vs[h]], state[h].astype(BF16)], axis=0))
             for h in heads]
        upd = [lax.dot_general(kl[h], v_ref[rows, vs[h]], (((0,), (0,)), ((), ())),
                               preferred_element_type=F32) for h in heads]
        state = [jnp.concatenate(
            [state[h][:, j * GLA_CHUNK:(j + 1) * GLA_CHUNK] * dec[h]
             for j in range(GLA_DV // GLA_CHUNK)], axis=1) + upd[h] for h in heads]
        for h in heads:
            inv_rms = lax.rsqrt(jnp.mean(o[h] * o[h], axis=-1, keepdims=True) + RMS_EPS)
            o_ref[rows, vs[h]] = (o[h] * inv_rms * g_ref[rows, vs[h]].astype(F32)).astype(BF16)
    for h in heads:
        state_ref[h] = state[h]


def _gla_chunk(q, k, v, gate, bcum, batch, seq):
    nc = seq // ROW_TILE
    blk = lambda w: pl.BlockSpec((ROW_TILE, w), lambda b, c: (b * nc + c, 0))
    return pl.pallas_call(
        _gla_chunk_kernel,
        grid=(batch, nc),
        in_specs=[blk(GLA_QK), blk(GLA_QK), blk(GLA_V), blk(GLA_V), blk(GLA_QK)],
        out_specs=blk(GLA_V),
        out_shape=jax.ShapeDtypeStruct((batch * seq, GLA_V), BF16),
        scratch_shapes=[pltpu.VMEM((GLA_HEADS, GLA_DK, GLA_DV), F32)],
        compiler_params=pltpu.CompilerParams(
            dimension_semantics=("parallel", "arbitrary"), vmem_limit_bytes=VMEM_LIMIT),
        name="gla_chunk",
    )(q, k, v, gate, bcum)


def _swa_attn_kernel(sink_ref, q_ref, kp_ref, kc_ref, vp_ref, vc_ref, o_ref):
    has_prev = (pl.program_id(1) > 0).astype(BF16)
    k_all = jnp.concatenate([kp_ref[...] * has_prev, kc_ref[...]], axis=0)
    v_all = jnp.concatenate([vp_ref[...] * has_prev, vc_ref[...]], axis=0)
    qi = lax.broadcasted_iota(jnp.int32, (SWA_BLOCK, SWA_BLOCK), 0)
    jj = lax.broadcasted_iota(jnp.int32, (SWA_BLOCK, SWA_BLOCK), 1)
    from_prev = jj > qi
    even_lanes = jj < SWA_HEAD_DIM
    one = jnp.ones((2 * SWA_BLOCK, LANES), BF16)
    pairs_per_kv = SWA_GROUP // 2
    for blk in range(SWA_Q_TILE // SWA_BLOCK):
        band = slice(blk * SWA_BLOCK, (blk + 2) * SWA_BLOCK)
        rows = slice(blk * SWA_BLOCK, (blk + 1) * SWA_BLOCK)
        k2, v2 = [], []
        for kv in range(SWA_KV_HEADS):
            cs = slice(kv * SWA_HEAD_DIM, (kv + 1) * SWA_HEAD_DIM)
            kh = k_all[band, cs]
            vh = v_all[band, cs]
            k2.append(jnp.concatenate([kh, kh], axis=1))
            v2.append(jnp.concatenate([vh, vh, one], axis=1))
        scores, maxes = [], []
        for kv in range(SWA_KV_HEADS):
            q_heads = []
            for j in range(pairs_per_kv):
                hp = kv * pairs_per_kv + j
                q_pair = q_ref[rows, hp * LANES:(hp + 1) * LANES]
                q_heads += [jnp.where(even_lanes, q_pair, jnp.zeros_like(q_pair)),
                            jnp.where(even_lanes, jnp.zeros_like(q_pair), q_pair)]
            s_grp = _dot_nt(jnp.concatenate(q_heads, axis=0), k2[kv])
            for g in range(SWA_GROUP):
                s2 = s_grp[g * SWA_BLOCK:(g + 1) * SWA_BLOCK]
                s = jnp.where(from_prev, s2[:, :SWA_BLOCK], s2[:, SWA_BLOCK:])
                scores.append(s)
                maxes.append(jnp.maximum(jnp.max(s, axis=-1, keepdims=True),
                                         sink_ref[kv * SWA_GROUP + g]))
        probs = []
        for h in range(SWA_Q_HEADS):
            p = jnp.exp(scores[h] - maxes[h])
            probs.append(jnp.concatenate([jnp.where(from_prev, p, 0.0),
                                          jnp.where(from_prev, 0.0, p)], axis=1).astype(BF16))
        pvs = []
        for kv in range(SWA_KV_HEADS):
            p_grp = jnp.concatenate(probs[kv * SWA_GROUP:(kv + 1) * SWA_GROUP], axis=0)
            pv_grp = _dot(p_grp, v2[kv])
            pvs += [pv_grp[g * SWA_BLOCK:(g + 1) * SWA_BLOCK] for g in range(SWA_GROUP)]
        for hp in range(SWA_Q_HEADS // 2):
            pv_e, pv_o = pvs[2 * hp], pvs[2 * hp + 1]
            num = jnp.where(even_lanes, pv_e[:, :LANES], pv_o[:, :LANES])
            den = jnp.where(even_lanes,
                            pv_e[:, LANES:] + jnp.exp(sink_ref[2 * hp] - maxes[2 * hp]),
                            pv_o[:, LANES:] + jnp.exp(sink_ref[2 * hp + 1] - maxes[2 * hp + 1]))
            o_ref[rows, hp * LANES:(hp + 1) * LANES] = (num / den).astype(BF16)


def _swa_attn(sinks, q, k, v, batch, seq):
    nt = seq // SWA_Q_TILE
    per = SWA_Q_TILE // SWA_BLOCK
    cur = lambda w: pl.BlockSpec((SWA_Q_TILE, w), lambda b, n, *_: (b * nt + n, 0))
    prev = lambda w: pl.BlockSpec(
        (SWA_BLOCK, w), lambda b, n, *_: (b * nt * per + jnp.maximum(n * per - 1, 0), 0))
    return pl.pallas_call(
        _swa_attn_kernel,
        grid_spec=pltpu.PrefetchScalarGridSpec(
            num_scalar_prefetch=1,
            grid=(batch, nt),
            in_specs=[cur(D_MODEL), prev(SWA_KV), cur(SWA_KV), prev(SWA_KV), cur(SWA_KV)],
            out_specs=cur(D_MODEL)),
        out_shape=jax.ShapeDtypeStruct((batch * seq, D_MODEL), BF16),
        compiler_params=pltpu.CompilerParams(
            dimension_semantics=("parallel", "parallel"), vmem_limit_bytes=VMEM_LIMIT),
        name="swa_attn",
    )(sinks, q, k, k, v, v)


def _post_mixer_kernel(*refs, has_out_bias, next_qkv):
    it = iter(refs)
    mix_ref, h_ref, p_ref, wo_ref = next(it), next(it), next(it), next(it)
    bo_ref = next(it) if has_out_bias else None
    g1_ref, b1_ref, wup_ref, wdn_ref, g2_ref, b2_ref, wg_ref, bg_ref, wp_ref = (
        next(it) for _ in range(9))
    wqkv_ref, bqkv_ref = (next(it), next(it)) if next_qkv else (None, None)
    out_ref = next(it)
    q_ref, k_ref, v_ref = (next(it), next(it), next(it)) if next_qkv else (None, None, None)

    half = POST_TILE // 2
    rows = [slice(0, half), slice(half, POST_TILE)]

    def out_proj(r):
        y = DEEPNORM_ALPHA * h_ref[r, :] + _dot(mix_ref[r, :], wo_ref[...])
        return y + bo_ref[...] if has_out_bias else y

    def mlp(h1b, acc, chunks):
        for c in chunks:
            u = jnp.maximum(_dot(h1b, wup_ref[:, c * FF_TILE:(c + 1) * FF_TILE]), 0.0)
            acc = acc + _dot((u * u).astype(BF16), wdn_ref[c * FF_TILE:(c + 1) * FF_TILE, :])
        return acc

    def ple(r, h2):
        z = _dot(h2.astype(BF16), wg_ref[...]) + bg_ref[...]
        gate = 1.0 / (1.0 + jnp.exp(-z))
        out = h2 + gate * _dot(p_ref[r, :].astype(BF16), wp_ref[...])
        out_ref[r, :] = out
        if next_qkv:
            ob = out.astype(BF16)
            kcol, vcol = D_MODEL, D_MODEL + SWA_KV
            q = _dot(ob, wqkv_ref[:, :kcol]) + bqkv_ref[:, :kcol]
            q_ref[r, :] = (q * (SWA_HEAD_DIM ** -0.5)).astype(BF16)
            k_ref[r, :] = (_dot(ob, wqkv_ref[:, kcol:vcol]) + bqkv_ref[:, kcol:vcol]).astype(BF16)
            v_ref[r, :] = (_dot(ob, wqkv_ref[:, vcol:]) + bqkv_ref[:, vcol:]).astype(BF16)

    n_ff = D_FF // FF_TILE
    y_a = out_proj(rows[0])
    y_b = out_proj(rows[1])
    h1_a = _layer_norm(y_a, g1_ref[...], b1_ref[...])
    h1b_a = h1_a.astype(BF16)
    acc_a = mlp(h1b_a, DEEPNORM_ALPHA * h1_a, range(0, 1))
    h1_b = _layer_norm(y_b, g1_ref[...], b1_ref[...])
    h1b_b = h1_b.astype(BF16)
    acc_a = mlp(h1b_a, acc_a, range(1, n_ff))
    acc_b = mlp(h1b_b, DEEPNORM_ALPHA * h1_b, range(0, 1))
    h2_a = _layer_norm(acc_a, g2_ref[...], b2_ref[...])
    acc_b = mlp(h1b_b, acc_b, range(1, n_ff))
    h2_b = _layer_norm(acc_b, g2_ref[...], b2_ref[...])
    ple(rows[0], h2_a)
    ple(rows[1], h2_b)


def _post_mixer(mix, h2d, p3, layer, weights, out_bias=None, qkv=None):
    t = h2d.shape[0]
    row = lambda w: pl.BlockSpec((POST_TILE, w), lambda i: (i, 0))
    wo, rest = weights[0], tuple(weights[1:])
    consts = (wo,) + ((out_bias,) if out_bias is not None else ()) + rest + (qkv or ())
    out_specs = [row(D_MODEL)]
    out_shape = [jax.ShapeDtypeStruct((t, D_MODEL), F32)]
    if qkv is not None:
        out_specs += [row(D_MODEL), row(SWA_KV), row(SWA_KV)]
        out_shape += [jax.ShapeDtypeStruct((t, D_MODEL), BF16),
                      jax.ShapeDtypeStruct((t, SWA_KV), BF16),
                      jax.ShapeDtypeStruct((t, SWA_KV), BF16)]
    return pl.pallas_call(
        functools.partial(_post_mixer_kernel, has_out_bias=out_bias is not None,
                          next_qkv=qkv is not None),
        grid=(t // POST_TILE,),
        in_specs=[row(D_MODEL), row(D_MODEL),
                  pl.BlockSpec((None, POST_TILE, PLE_DIM), lambda i: (layer, i, 0))]
                 + [_resident(w.shape) for w in consts],
        out_specs=out_specs,
        out_shape=out_shape,
        compiler_params=pltpu.CompilerParams(
            dimension_semantics=("parallel",), vmem_limit_bytes=VMEM_LIMIT),
        name=f"post_mixer_{layer}",
    )(mix, h2d, p3, *consts)


def kernel(x, p, gla_w_in, gla_w_gk_up, gla_b_gk, gla_norm_g, gla_w_out, swa_w_qkv, swa_b_qkv,
           swa_sinks, swa_w_out, swa_b_out, mlp_w_up, mlp_w_down, ln1_g, ln1_b, ln2_g, ln2_b,
           ple_w_proj, ple_w_gate, ple_b_gate):
    batch, seq, d = x.shape
    t = batch * seq
    x2 = x.reshape(t, d)
    p3 = p.reshape(DEPTH, t, PLE_DIM)
    vec = lambda a: a.reshape(1, -1).astype(F32)

    def post_weights(layer, wo):
        return (wo.astype(BF16), vec(ln1_g[layer]), vec(ln1_b[layer]),
                mlp_w_up[layer].astype(BF16), mlp_w_down[layer].astype(BF16),
                vec(ln2_g[layer]), vec(ln2_b[layer]), ple_w_gate[layer].astype(BF16),
                vec(ple_b_gate[layer]), ple_w_proj[layer].astype(BF16))

    norm_g = vec(jnp.tile(gla_norm_g[0], GLA_HEADS))
    q, k, v, gate, bcum = _gla_inproj(x2, gla_w_in[0].astype(BF16), gla_w_gk_up[0].astype(BF16),
                                      vec(gla_b_gk[0]), norm_g)
    mix = _gla_chunk(q, k, v, gate, bcum, batch, seq)
    h, q, k, v = _post_mixer(mix, x2, p3, 0, post_weights(0, gla_w_out[0]),
                             qkv=(swa_w_qkv[0].astype(BF16), vec(swa_b_qkv[0])))

    mix = _swa_attn(swa_sinks[0].astype(F32), q, k, v, batch, seq)
    (h,) = _post_mixer(mix, h, p3, 1, post_weights(1, swa_w_out[0]), out_bias=vec(swa_b_out[0]))
    return h.reshape(batch, seq, d)
```

```python
import functools

import jax
import jax.numpy as jnp
from jax import lax
from jax.experimental import pallas as pl
from jax.experimental.pallas import tpu as pltpu

D_MODEL = 1024
DEPTH = 2
PLE_DIM = 256

GLA_HEADS = 4
GLA_DK = 128
GLA_DV = 256
GLA_LOWRANK = 16
GLA_TAU = 16.0
GLA_QK = GLA_HEADS * GLA_DK
GLA_V = GLA_HEADS * GLA_DV

SWA_HEAD_DIM = 64
SWA_Q_HEADS = 16
SWA_KV_HEADS = 4
SWA_GROUP = SWA_Q_HEADS // SWA_KV_HEADS
SWA_BLOCK = 128
SWA_Q_TILE = 256
SWA_KV = SWA_KV_HEADS * SWA_HEAD_DIM

D_FF = 4 * D_MODEL
DEEPNORM_ALPHA = (2.0 * DEPTH) ** 0.25
LN_EPS = 1e-5
RMS_EPS = 1e-5

LANES = 128
GLA_CHUNK = 128
ROW_TILE = 512
POST_TILE = 512
FF_TILE = 1024
VMEM_LIMIT = 60 * 1024 * 1024

BF16 = jnp.bfloat16
F32 = jnp.float32


def _dot(a, b):
    return jnp.dot(a, b, preferred_element_type=F32)


def _dot_nt(a, b):
    return lax.dot_general(a, b, (((1,), (1,)), ((), ())), preferred_element_type=F32)


def _resident(shape, layer=None):
    if layer is None:
        return pl.BlockSpec(shape, lambda *_: (0, 0), pipeline_mode=pl.Buffered(1))
    return pl.BlockSpec((None,) + tuple(shape[1:]), lambda *_: (layer, 0, 0),
                        pipeline_mode=pl.Buffered(1))


def _layer_norm(y, g, b):
    mu = jnp.mean(y, axis=-1, keepdims=True)
    yc = y - mu
    var = jnp.mean(yc * yc, axis=-1, keepdims=True)
    return yc * lax.rsqrt(var + LN_EPS) * g + b


def _gla_kernel(x_ref, w_ref, wup_ref, bgk_ref, ng_ref, o_ref,
                q_s, k_s, v_s, g_s, b_s, state_ref, *, tiles_per_seq):
    i = pl.program_id(0)
    wr = i % 2
    rd = 1 - wr

    @pl.when(i == 0)
    def _():
        q_s[1] = jnp.zeros(q_s.shape[1:], q_s.dtype)
        k_s[1] = jnp.zeros(k_s.shape[1:], k_s.dtype)
        v_s[1] = jnp.zeros(v_s.shape[1:], v_s.dtype)
        g_s[1] = jnp.zeros(g_s.shape[1:], g_s.dtype)
        b_s[1] = jnp.zeros(b_s.shape[1:], b_s.dtype)
        state_ref[...] = jnp.zeros_like(state_ref)

    c_k, c_v, c_r, c_gl = GLA_QK, 2 * GLA_QK, 2 * GLA_QK + GLA_V, 2 * GLA_QK + 2 * GLA_V
    row = lax.broadcasted_iota(jnp.int32, (GLA_CHUNK, GLA_CHUNK), 0)
    col = lax.broadcasted_iota(jnp.int32, (GLA_CHUNK, GLA_CHUNK), 1)
    causal = col <= row
    tril2 = jnp.concatenate([causal.astype(BF16)] * 2, axis=1)
    heads = range(GLA_HEADS)
    ks = [slice(h * GLA_DK, (h + 1) * GLA_DK) for h in heads]
    vs = [slice(h * GLA_DV, (h + 1) * GLA_DV) for h in heads]

    fresh = (i - 1) % tiles_per_seq == 0
    state = [jnp.where(fresh, 0.0, state_ref[h]) for h in heads]

    def prepare(c):
        rows = slice(c * GLA_CHUNK, (c + 1) * GLA_CHUNK)
        qd, kl, dec, att = [], [], [], []
        for h in heads:
            b = b_s[rd, rows, ks[h]]
            d_last = jnp.exp(b[GLA_CHUNK - 1:GLA_CHUNK, :])
            k_dec = k_s[rd, rows, ks[h]] * jnp.exp(-b).astype(BF16)
            qd.append(q_s[rd, rows, ks[h]] * jnp.exp(b).astype(BF16))
            kl.append(k_dec * d_last.astype(BF16))
            dec.append(jnp.broadcast_to(d_last, (GLA_CHUNK, GLA_DK)).T)
            att.append(_dot_nt(qd[h], k_dec))
        return rows, qd, kl, dec, att

    def contract(prep, state):
        rows, qd, kl, _, att = prep
        o, upd = [], []
        for h in heads:
            v = v_s[rd, rows, vs[h]]
            lhs = jnp.concatenate([jnp.where(causal, att[h], 0.0).astype(BF16), qd[h]], axis=1)
            o.append(_dot(lhs, jnp.concatenate([v, state[h].astype(BF16)], axis=0)))
            upd.append(lax.dot_general(kl[h], v, (((0,), (0,)), ((), ())),
                                       preferred_element_type=F32))
        return o, upd

    def finish(prep, o, upd, state):
        rows, _, _, dec, _ = prep
        state = [jnp.concatenate(
            [state[h][:, j * GLA_CHUNK:(j + 1) * GLA_CHUNK] * dec[h]
             for j in range(GLA_DV // GLA_CHUNK)], axis=1) + upd[h] for h in heads]
        for h in heads:
            inv_rms = lax.rsqrt(jnp.mean(o[h] * o[h], axis=-1, keepdims=True) + RMS_EPS)
            o_ref[rows, vs[h]] = (o[h] * inv_rms * g_s[rd, rows, vs[h]].astype(F32)).astype(BF16)
        return state

    half_v = GLA_V // 2
    xb = x_ref[...].astype(BF16)
    gl = _dot(xb, w_ref[:, c_gl:]).astype(BF16)
    prep = prepare(0)
    r0 = _dot(xb, w_ref[:, c_r:c_r + half_v])
    o, upd = contract(prep, state)
    r1 = _dot(xb, w_ref[:, c_r + half_v:c_gl])
    z = _dot(gl, wup_ref[...]) + bgk_ref[...]
    state = finish(prep, o, upd, state)
    r = jnp.concatenate([r0, r1], axis=1)
    g_s[wr] = (r / (1.0 + jnp.exp(-r)) * ng_ref[...]).astype(BF16)
    prep = prepare(1)
    q_s[wr] = (_dot(xb, w_ref[:, :c_k]) * (GLA_DK ** -0.5)).astype(BF16)
    o, upd = contract(prep, state)
    k_s[wr] = _dot(xb, w_ref[:, c_k:c_v]).astype(BF16)
    state = finish(prep, o, upd, state)
    log_a = (jnp.minimum(z, 0.0) - jnp.log1p(jnp.exp(-jnp.abs(z)))) * (1.0 / GLA_TAU)
    hi = log_a.astype(BF16)
    lo = (log_a - hi.astype(F32)).astype(BF16)
    prep = prepare(2)
    v_s[wr, :, :half_v] = _dot(xb, w_ref[:, c_v:c_v + half_v]).astype(BF16)
    o, upd = contract(prep, state)
    v_s[wr, :, half_v:] = _dot(xb, w_ref[:, c_v + half_v:c_r]).astype(BF16)
    state = finish(prep, o, upd, state)
    prep = prepare(3)
    for c in range(ROW_TILE // GLA_CHUNK):
        rows = slice(c * GLA_CHUNK, (c + 1) * GLA_CHUNK)
        b_s[wr, rows, :] = _dot(tril2, jnp.concatenate([hi[rows], lo[rows]], axis=0))
    o, upd = contract(prep, state)
    state = finish(prep, o, upd, state)
    for h in heads:
        state_ref[h] = state[h]


def _gla(x2, w_in, wup, bgk, ng, seq):
    t = x2.shape[0]
    n = t // ROW_TILE
    assert ROW_TILE // GLA_CHUNK == 4 and seq % ROW_TILE == 0
    return pl.pallas_call(
        functools.partial(_gla_kernel, tiles_per_seq=seq // ROW_TILE),
        grid=(n + 1,),
        in_specs=[pl.BlockSpec((ROW_TILE, D_MODEL), lambda i: (jnp.minimum(i, n - 1), 0)),
                  _resident(w_in.shape), _resident(wup.shape), _resident(bgk.shape),
                  _resident(ng.shape)],
        out_specs=pl.BlockSpec((ROW_TILE, GLA_V), lambda i: (jnp.maximum(i - 1, 0), 0)),
        out_shape=jax.ShapeDtypeStruct((t, GLA_V), BF16),
        scratch_shapes=[pltpu.VMEM((2, ROW_TILE, GLA_QK), BF16),
                        pltpu.VMEM((2, ROW_TILE, GLA_QK), BF16),
                        pltpu.VMEM((2, ROW_TILE, GLA_V), BF16),
                        pltpu.VMEM((2, ROW_TILE, GLA_V), BF16),
                        pltpu.VMEM((2, ROW_TILE, GLA_QK), F32),
                        pltpu.VMEM((GLA_HEADS, GLA_DK, GLA_DV), F32)],
        compiler_params=pltpu.CompilerParams(
            dimension_semantics=("arbitrary",), vmem_limit_bytes=VMEM_LIMIT),
        name="gla",
    )(x2, w_in, wup, bgk, ng)


def _swa_attn_kernel(sink_ref, q_ref, kp_ref, kc_ref, vp_ref, vc_ref, o_ref):
    has_prev = (pl.program_id(1) > 0).astype(BF16)
    k_all = jnp.concatenate([kp_ref[...] * has_prev, kc_ref[...]], axis=0)
    v_all = jnp.concatenate([vp_ref[...] * has_prev, vc_ref[...]], axis=0)
    grp = SWA_GROUP * SWA_BLOCK
    kj = lax.broadcasted_iota(jnp.int32, (SWA_BLOCK, grp), 0)
    qi = lax.broadcasted_iota(jnp.int32, (SWA_BLOCK, grp), 1) % SWA_BLOCK
    from_prev = kj > qi
    lane = lax.broadcasted_iota(jnp.int32, (SWA_BLOCK, LANES), 1)
    even_lanes = lane < SWA_HEAD_DIM
    one = jnp.ones((2 * SWA_BLOCK, SWA_HEAD_DIM), BF16)
    pairs_per_kv = SWA_GROUP // 2
    sink_rows = [jnp.concatenate(
        [jnp.full((1, SWA_BLOCK), sink_ref[kv * SWA_GROUP + g], F32) for g in range(SWA_GROUP)],
        axis=1) for kv in range(SWA_KV_HEADS)]
    for blk in range(SWA_Q_TILE // SWA_BLOCK):
        band = slice(blk * SWA_BLOCK, (blk + 2) * SWA_BLOCK)
        rows = slice(blk * SWA_BLOCK, (blk + 1) * SWA_BLOCK)
        scores, maxes = [], []
        for kv in range(SWA_KV_HEADS):
            cs = slice(kv * SWA_HEAD_DIM, (kv + 1) * SWA_HEAD_DIM)
            kh = k_all[band, cs]
            k2 = jnp.concatenate([kh, kh], axis=1)
            q_heads = []
            for j in range(pairs_per_kv):
                hp = kv * pairs_per_kv + j
                q_pair = q_ref[rows, hp * LANES:(hp + 1) * LANES]
                q_heads += [jnp.where(even_lanes, q_pair, jnp.zeros_like(q_pair)),
                            jnp.where(even_lanes, jnp.zeros_like(q_pair), q_pair)]
            s2 = _dot_nt(k2, jnp.concatenate(q_heads, axis=0))
            s = jnp.where(from_prev, s2[:SWA_BLOCK], s2[SWA_BLOCK:])
            scores.append(s)
            maxes.append(jnp.maximum(jnp.max(s, axis=0, keepdims=True), sink_rows[kv]))
        probs = []
        for kv in range(SWA_KV_HEADS):
            p = jnp.exp(scores[kv] - maxes[kv])
            probs.append(jnp.concatenate([jnp.where(from_prev, p, 0.0),
                                          jnp.where(from_prev, 0.0, p)], axis=0).astype(BF16))
        outs = []
        for kv in range(SWA_KV_HEADS):
            cs = slice(kv * SWA_HEAD_DIM, (kv + 1) * SWA_HEAD_DIM)
            v_ext = jnp.concatenate([v_all[band, cs], one], axis=1)
            pv = lax.dot_general(v_ext, probs[kv], (((0,), (0,)), ((), ())),
                                 preferred_element_type=F32)
            den = pv[SWA_HEAD_DIM:] + jnp.exp(sink_rows[kv] - maxes[kv])
            outs.append(pv[:SWA_HEAD_DIM] / den)
        for hp in range(SWA_Q_HEADS // 2):
            o_kv = outs[hp // pairs_per_kv]
            g0 = (hp % pairs_per_kv) * 2 * SWA_BLOCK
            pair = jnp.concatenate([o_kv[:, g0:g0 + SWA_BLOCK],
                                    o_kv[:, g0 + SWA_BLOCK:g0 + 2 * SWA_BLOCK]], axis=0)
            o_ref[rows, hp * LANES:(hp + 1) * LANES] = pair.T.astype(BF16)


def _swa_attn(sinks, q, k, v, batch, seq):
    nt = seq // SWA_Q_TILE
    per = SWA_Q_TILE // SWA_BLOCK
    cur = lambda w: pl.BlockSpec((SWA_Q_TILE, w), lambda b, n, *_: (b * nt + n, 0))
    prev = lambda w: pl.BlockSpec(
        (SWA_BLOCK, w), lambda b, n, *_: (b * nt * per + jnp.maximum(n * per - 1, 0), 0))
    return pl.pallas_call(
        _swa_attn_kernel,
        grid_spec=pltpu.PrefetchScalarGridSpec(
            num_scalar_prefetch=1,
            grid=(batch, nt),
            in_specs=[cur(D_MODEL), prev(SWA_KV), cur(SWA_KV), prev(SWA_KV), cur(SWA_KV)],
            out_specs=cur(D_MODEL)),
        out_shape=jax.ShapeDtypeStruct((batch * seq, D_MODEL), BF16),
        compiler_params=pltpu.CompilerParams(
            dimension_semantics=("parallel", "parallel"), vmem_limit_bytes=VMEM_LIMIT),
        name="swa_attn",
    )(sinks, q, k, k, v, v)


def _post_mixer_kernel(*refs, has_out_bias, next_qkv):
    it = iter(refs)
    mix_ref, h_ref, p_ref, wo_ref = next(it), next(it), next(it), next(it)
    bo_ref = next(it) if has_out_bias else None
    g1_ref, b1_ref, wup_ref, wdn_ref, g2_ref, b2_ref, wg_ref, bg_ref, wp_ref = (
        next(it) for _ in range(9))
    wqkv_ref, bqkv_ref = (next(it), next(it)) if next_qkv else (None, None)
    out_ref = next(it)
    q_ref, k_ref, v_ref = (next(it), next(it), next(it)) if next_qkv else (None, None, None)

    half = POST_TILE // 2
    rows = [slice(0, half), slice(half, POST_TILE)]

    def out_proj(r):
        y = DEEPNORM_ALPHA * h_ref[r, :] + _dot(mix_ref[r, :], wo_ref[...])
        return y + bo_ref[...] if has_out_bias else y

    def mlp(h1b, acc, chunks):
        for c in chunks:
            u = jnp.maximum(_dot(h1b, wup_ref[:, c * FF_TILE:(c + 1) * FF_TILE]), 0.0)
            acc = acc + _dot((u * u).astype(BF16), wdn_ref[c * FF_TILE:(c + 1) * FF_TILE, :])
        return acc

    def ple(r, h2):
        z = _dot(h2.astype(BF16), wg_ref[...]) + bg_ref[...]
        gate = 1.0 / (1.0 + jnp.exp(-z))
        out = h2 + gate * _dot(p_ref[r, :].astype(BF16), wp_ref[...])
        out_ref[r, :] = out
        if next_qkv:
            ob = out.astype(BF16)
            kcol, vcol = D_MODEL, D_MODEL + SWA_KV
            q = _dot(ob, wqkv_ref[:, :kcol]) + bqkv_ref[:, :kcol]
            q_ref[r, :] = (q * (SWA_HEAD_DIM ** -0.5)).astype(BF16)
            k_ref[r, :] = (_dot(ob, wqkv_ref[:, kcol:vcol]) + bqkv_ref[:, kcol:vcol]).astype(BF16)
            v_ref[r, :] = (_dot(ob, wqkv_ref[:, vcol:]) + bqkv_ref[:, vcol:]).astype(BF16)

    n_ff = D_FF // FF_TILE
    y_a = out_proj(rows[0])
    y_b = out_proj(rows[1])
    h1_a = _layer_norm(y_a, g1_ref[...], b1_ref[...])
    h1b_a = h1_a.astype(BF16)
    acc_a = mlp(h1b_a, DEEPNORM_ALPHA * h1_a, range(0, 1))
    h1_b = _layer_norm(y_b, g1_ref[...], b1_ref[...])
    h1b_b = h1_b.astype(BF16)
    acc_a = mlp(h1b_a, acc_a, range(1, n_ff))
    acc_b = mlp(h1b_b, DEEPNORM_ALPHA * h1_b, range(0, 1))
    h2_a = _layer_norm(acc_a, g2_ref[...], b2_ref[...])
    acc_b = mlp(h1b_b, acc_b, range(1, n_ff))
    h2_b = _layer_norm(acc_b, g2_ref[...], b2_ref[...])
    ple(rows[0], h2_a)
    ple(rows[1], h2_b)


def _post_mixer(mix, h2d, p3, layer, weights, out_bias=None, qkv=None):
    t = h2d.shape[0]
    row = lambda w: pl.BlockSpec((POST_TILE, w), lambda i: (i, 0))
    wo, rest = weights[0], tuple(weights[1:])
    consts = (wo,) + ((out_bias,) if out_bias is not None else ()) + rest + (qkv or ())
    out_specs = [row(D_MODEL)]
    out_shape = [jax.ShapeDtypeStruct((t, D_MODEL), F32)]
    if qkv is not None:
        out_specs += [row(D_MODEL), row(SWA_KV), row(SWA_KV)]
        out_shape += [jax.ShapeDtypeStruct((t, D_MODEL), BF16),
                      jax.ShapeDtypeStruct((t, SWA_KV), BF16),
                      jax.ShapeDtypeStruct((t, SWA_KV), BF16)]
    return pl.pallas_call(
        functools.partial(_post_mixer_kernel, has_out_bias=out_bias is not None,
                          next_qkv=qkv is not None),
        grid=(t // POST_TILE,),
        in_specs=[row(D_MODEL), row(D_MODEL),
                  pl.BlockSpec((None, POST_TILE, PLE_DIM), lambda i: (layer, i, 0))]
                 + [_resident(w.shape, layer if w.ndim == 3 else None) for w in consts],
        out_specs=out_specs,
        out_shape=out_shape,
        compiler_params=pltpu.CompilerParams(
            dimension_semantics=("parallel",), vmem_limit_bytes=VMEM_LIMIT),
        name=f"post_mixer_{layer}",
    )(mix, h2d, p3, *consts)


def kernel(x, p, gla_w_in, gla_w_gk_up, gla_b_gk, gla_norm_g, gla_w_out, swa_w_qkv, swa_b_qkv,
           swa_sinks, swa_w_out, swa_b_out, mlp_w_up, mlp_w_down, ln1_g, ln1_b, ln2_g, ln2_b,
           ple_w_proj, ple_w_gate, ple_b_gate):
    batch, seq, d = x.shape
    t = batch * seq
    x2 = x.reshape(t, d)
    p3 = p.reshape(DEPTH, t, PLE_DIM)
    vec = lambda a: a.reshape(1, -1).astype(F32)

    w_up, w_down = mlp_w_up.astype(BF16), mlp_w_down.astype(BF16)
    w_gate, w_proj = ple_w_gate.astype(BF16), ple_w_proj.astype(BF16)

    def post_weights(layer, wo):
        return (wo.astype(BF16), vec(ln1_g[layer]), vec(ln1_b[layer]), w_up, w_down,
                vec(ln2_g[layer]), vec(ln2_b[layer]), w_gate, vec(ple_b_gate[layer]), w_proj)

    norm_g = vec(jnp.tile(gla_norm_g[0], GLA_HEADS))
    mix = _gla(x2, gla_w_in[0].astype(BF16), gla_w_gk_up[0].astype(BF16), vec(gla_b_gk[0]),
               norm_g, seq)
    h, q, k, v = _post_mixer(mix, x2, p3, 0, post_weights(0, gla_w_out[0]),
                             qkv=(swa_w_qkv[0].astype(BF16), vec(swa_b_qkv[0])))

    mix = _swa_attn(swa_sinks[0].astype(F32), q, k, v, batch, seq)
    (h,) = _post_mixer(mix, h, p3, 1, post_weights(1, swa_w_out[0]), out_bias=vec(swa_b_out[0]))
    return h.reshape(batch, seq, d)
```

```python
import functools

import jax
import jax.numpy as jnp
from jax import lax
from jax.experimental import pallas as pl
from jax.experimental.pallas import tpu as pltpu

D_MODEL = 1024
DEPTH = 2
PLE_DIM = 256

GLA_HEADS = 4
GLA_DK = 128
GLA_DV = 256
GLA_LOWRANK = 16
GLA_TAU = 16.0
GLA_QK = GLA_HEADS * GLA_DK
GLA_V = GLA_HEADS * GLA_DV

SWA_HEAD_DIM = 64
SWA_Q_HEADS = 16
SWA_KV_HEADS = 4
SWA_GROUP = SWA_Q_HEADS // SWA_KV_HEADS
SWA_BLOCK = 128
SWA_Q_TILE = 512
SWA_KV = SWA_KV_HEADS * SWA_HEAD_DIM

D_FF = 4 * D_MODEL
DEEPNORM_ALPHA = (2.0 * DEPTH) ** 0.25
LN_EPS = 1e-5
RMS_EPS = 1e-5

LANES = 128
GLA_CHUNK = 128
ROW_TILE = 512
POST_TILE = 512
FF_TILE = 1024
VMEM_LIMIT = 60 * 1024 * 1024

BF16 = jnp.bfloat16
F32 = jnp.float32


def _dot(a, b):
    return jnp.dot(a, b, preferred_element_type=F32)


def _dot_nt(a, b):
    return lax.dot_general(a, b, (((1,), (1,)), ((), ())), preferred_element_type=F32)


def _resident(shape, layer=None):
    if layer is None:
        return pl.BlockSpec(shape, lambda *_: (0, 0), pipeline_mode=pl.Buffered(1))
    return pl.BlockSpec((None,) + tuple(shape[1:]), lambda *_: (layer, 0, 0),
                        pipeline_mode=pl.Buffered(1))


def _layer_norm(y, g, b):
    mu = jnp.mean(y, axis=-1, keepdims=True)
    yc = y - mu
    var = jnp.mean(yc * yc, axis=-1, keepdims=True)
    return yc * lax.rsqrt(var + LN_EPS) * g + b


def _gla_kernel(x_ref, w_ref, wup_ref, bgk_ref, ng_ref, o_ref,
                q_s, k_s, v_s, g_s, b_s, state_ref, *, tiles_per_seq):
    i = pl.program_id(0)
    wr = i % 2
    rd = 1 - wr

    @pl.when(i == 0)
    def _():
        q_s[1] = jnp.zeros(q_s.shape[1:], q_s.dtype)
        k_s[1] = jnp.zeros(k_s.shape[1:], k_s.dtype)
        v_s[1] = jnp.zeros(v_s.shape[1:], v_s.dtype)
        g_s[1] = jnp.zeros(g_s.shape[1:], g_s.dtype)
        b_s[1] = jnp.zeros(b_s.shape[1:], b_s.dtype)
        state_ref[...] = jnp.zeros_like(state_ref)

    c_k, c_v, c_r, c_gl = GLA_QK, 2 * GLA_QK, 2 * GLA_QK + GLA_V, 2 * GLA_QK + 2 * GLA_V
    row = lax.broadcasted_iota(jnp.int32, (GLA_CHUNK, GLA_CHUNK), 0)
    col = lax.broadcasted_iota(jnp.int32, (GLA_CHUNK, GLA_CHUNK), 1)
    causal = col <= row
    tril2 = jnp.concatenate([causal.astype(BF16)] * 2, axis=1)
    heads = range(GLA_HEADS)
    ks = [slice(h * GLA_DK, (h + 1) * GLA_DK) for h in heads]
    vs = [slice(h * GLA_DV, (h + 1) * GLA_DV) for h in heads]

    fresh = (i - 1) % tiles_per_seq == 0
    state = [jnp.where(fresh, 0.0, state_ref[h]) for h in heads]

    n_chunks = ROW_TILE // GLA_CHUNK
    chunk_rows = [slice(c * GLA_CHUNK, (c + 1) * GLA_CHUNK) for c in range(n_chunks)]

    def prepare(c):
        out = []
        for h in heads:
            b = b_s[rd, chunk_rows[c], ks[h]]
            b_last = b[GLA_CHUNK - 1:GLA_CHUNK, :]
            half = 0.5 * b_last
            e_half = jnp.exp(half).astype(BF16)
            bq = b - half
            qd = q_s[rd, chunk_rows[c], ks[h]] * jnp.exp(bq).astype(BF16)
            kd = k_s[rd, chunk_rows[c], ks[h]] * jnp.exp(-bq).astype(BF16)
            dec = jnp.broadcast_to(jnp.exp(b_last), (GLA_CHUNK, GLA_DK)).T
            out.append(dict(att=_dot_nt(qd, kd), q_in=qd * e_half,
                            kl=kd * e_half, dec=dec))
        return out

    half_v = GLA_V // 2
    xb = x_ref[...].astype(BF16)
    gl = _dot(xb, w_ref[:, c_gl:]).astype(BF16)
    prep = [prepare(0)]
    r0 = _dot(xb, w_ref[:, c_r:c_r + half_v])
    prep.append(prepare(1))
    r1 = _dot(xb, w_ref[:, c_r + half_v:c_gl])
    z = _dot(gl, wup_ref[...]) + bgk_ref[...]
    prep.append(prepare(2))
    q_s[wr] = (_dot(xb, w_ref[:, :c_k]) * (GLA_DK ** -0.5)).astype(BF16)
    prep.append(prepare(3))
    k_s[wr] = _dot(xb, w_ref[:, c_k:c_v]).astype(BF16)

    upd = [[lax.dot_general(prep[c][h]["kl"], v_s[rd, chunk_rows[c], vs[h]],
                            (((0,), (0,)), ((), ())), preferred_element_type=F32)
            for h in heads] for c in range(n_chunks)]
    r = jnp.concatenate([r0, r1], axis=1)
    g_s[wr] = (r / (1.0 + jnp.exp(-r)) * ng_ref[...]).astype(BF16)
    states = [state]
    for c in range(n_chunks):
        states.append([jnp.concatenate(
            [states[c][h][:, j * GLA_CHUNK:(j + 1) * GLA_CHUNK] * prep[c][h]["dec"]
             for j in range(GLA_DV // GLA_CHUNK)], axis=1) + upd[c][h] for h in heads])
    v_s[wr, :, :half_v] = _dot(xb, w_ref[:, c_v:c_v + half_v]).astype(BF16)

    o = []
    for c in range(n_chunks):
        o.append([_dot(
            jnp.concatenate([jnp.where(causal, prep[c][h]["att"], 0.0).astype(BF16),
                             prep[c][h]["q_in"]], axis=1),
            jnp.concatenate([v_s[rd, chunk_rows[c], vs[h]], states[c][h].astype(BF16)], axis=0))
            for h in heads])
        if c == 1:
            v_s[wr, :, half_v:] = _dot(xb, w_ref[:, c_v + half_v:c_r]).astype(BF16)
    log_a = (jnp.minimum(z, 0.0) - jnp.log1p(jnp.exp(-jnp.abs(z)))) * (1.0 / GLA_TAU)
    hi = log_a.astype(BF16)
    lo = (log_a - hi.astype(F32)).astype(BF16)
    for c in range(n_chunks):
        b_s[wr, chunk_rows[c], :] = _dot(
            tril2, jnp.concatenate([hi[chunk_rows[c]], lo[chunk_rows[c]]], axis=0))
    for c in range(n_chunks):
        for h in heads:
            inv_rms = lax.rsqrt(jnp.mean(o[c][h] * o[c][h], axis=-1, keepdims=True) + RMS_EPS)
            gate = g_s[rd, chunk_rows[c], vs[h]].astype(F32)
            o_ref[chunk_rows[c], vs[h]] = (o[c][h] * inv_rms * gate).astype(BF16)
    state = states[n_chunks]
    for h in heads:
        state_ref[h] = state[h]


def _gla(x2, w_in, wup, bgk, ng, seq):
    t = x2.shape[0]
    n = t // ROW_TILE
    assert ROW_TILE // GLA_CHUNK == 4 and seq % ROW_TILE == 0
    return pl.pallas_call(
        functools.partial(_gla_kernel, tiles_per_seq=seq // ROW_TILE),
        grid=(n + 1,),
        in_specs=[pl.BlockSpec((ROW_TILE, D_MODEL), lambda i: (jnp.minimum(i, n - 1), 0)),
                  _resident(w_in.shape), _resident(wup.shape), _resident(bgk.shape),
                  _resident(ng.shape)],
        out_specs=pl.BlockSpec((ROW_TILE, GLA_V), lambda i: (jnp.maximum(i - 1, 0), 0)),
        out_shape=jax.ShapeDtypeStruct((t, GLA_V), BF16),
        scratch_shapes=[pltpu.VMEM((2, ROW_TILE, GLA_QK), BF16),
                        pltpu.VMEM((2, ROW_TILE, GLA_QK), BF16),
                        pltpu.VMEM((2, ROW_TILE, GLA_V), BF16),
                        pltpu.VMEM((2, ROW_TILE, GLA_V), BF16),
                        pltpu.VMEM((2, ROW_TILE, GLA_QK), F32),
                        pltpu.VMEM((GLA_HEADS, GLA_DK, GLA_DV), F32)],
        compiler_params=pltpu.CompilerParams(
            dimension_semantics=("arbitrary",), vmem_limit_bytes=VMEM_LIMIT),
        name="gla",
    )(x2, w_in, wup, bgk, ng)


def _swa_attn_kernel(sink_ref, q_ref, kp_ref, kc_ref, vp_ref, vc_ref, o_ref):
    has_prev = (pl.program_id(1) > 0).astype(BF16)
    k_all = jnp.concatenate([kp_ref[...] * has_prev, kc_ref[...]], axis=0)
    v_all = jnp.concatenate([vp_ref[...] * has_prev, vc_ref[...]], axis=0)
    grp = SWA_GROUP * SWA_BLOCK
    kj = lax.broadcasted_iota(jnp.int32, (SWA_BLOCK, grp), 0)
    qi = lax.broadcasted_iota(jnp.int32, (SWA_BLOCK, grp), 1) % SWA_BLOCK
    from_prev = kj > qi
    lane = lax.broadcasted_iota(jnp.int32, (SWA_BLOCK, LANES), 1)
    even_lanes = lane < SWA_HEAD_DIM
    one = jnp.ones((2 * SWA_BLOCK, SWA_HEAD_DIM), BF16)
    pairs_per_kv = SWA_GROUP // 2
    sink_rows = [jnp.concatenate(
        [jnp.full((1, SWA_BLOCK), sink_ref[kv * SWA_GROUP + g], F32) for g in range(SWA_GROUP)],
        axis=1) for kv in range(SWA_KV_HEADS)]
    units = [(blk, kv) for blk in range(SWA_Q_TILE // SWA_BLOCK) for kv in range(SWA_KV_HEADS)]
    band = lambda blk: slice(blk * SWA_BLOCK, (blk + 2) * SWA_BLOCK)
    rows = lambda blk: slice(blk * SWA_BLOCK, (blk + 1) * SWA_BLOCK)
    heads = lambda kv: slice(kv * SWA_HEAD_DIM, (kv + 1) * SWA_HEAD_DIM)
    scores, maxes = [], []
    for blk, kv in units:
        kh = k_all[band(blk), heads(kv)]
        k2 = jnp.concatenate([kh, kh], axis=1)
        q_heads = []
        for j in range(pairs_per_kv):
            hp = kv * pairs_per_kv + j
            q_pair = q_ref[rows(blk), hp * LANES:(hp + 1) * LANES]
            q_heads += [jnp.where(even_lanes, q_pair, jnp.zeros_like(q_pair)),
                        jnp.where(even_lanes, jnp.zeros_like(q_pair), q_pair)]
        s2 = _dot_nt(k2, jnp.concatenate(q_heads, axis=0))
        s = jnp.where(from_prev, s2[:SWA_BLOCK], s2[SWA_BLOCK:])
        scores.append(s)
        maxes.append(jnp.maximum(jnp.max(s, axis=0, keepdims=True), sink_rows[kv]))
    probs = []
    for u in range(len(units)):
        p = jnp.exp(scores[u] - maxes[u])
        probs.append(jnp.concatenate([jnp.where(from_prev, p, 0.0),
                                      jnp.where(from_prev, 0.0, p)], axis=0).astype(BF16))
    outs = []
    for u, (blk, kv) in enumerate(units):
        v_ext = jnp.concatenate([v_all[band(blk), heads(kv)], one], axis=1)
        pv = lax.dot_general(v_ext, probs[u], (((0,), (0,)), ((), ())),
                             preferred_element_type=F32)
        den = pv[SWA_HEAD_DIM:] + jnp.exp(sink_rows[kv] - maxes[u])
        outs.append(pv[:SWA_HEAD_DIM] / den)
    for u, (blk, kv) in enumerate(units):
        for j in range(pairs_per_kv):
            hp = kv * pairs_per_kv + j
            g0 = j * 2 * SWA_BLOCK
            pair = jnp.concatenate([outs[u][:, g0:g0 + SWA_BLOCK],
                                    outs[u][:, g0 + SWA_BLOCK:g0 + 2 * SWA_BLOCK]], axis=0)
            o_ref[rows(blk), hp * LANES:(hp + 1) * LANES] = pair.T.astype(BF16)


def _swa_attn(sinks, q, k, v, batch, seq):
    nt = seq // SWA_Q_TILE
    per = SWA_Q_TILE // SWA_BLOCK
    cur = lambda w: pl.BlockSpec((SWA_Q_TILE, w), lambda b, n, *_: (b * nt + n, 0))
    prev = lambda w: pl.BlockSpec(
        (SWA_BLOCK, w), lambda b, n, *_: (b * nt * per + jnp.maximum(n * per - 1, 0), 0))
    return pl.pallas_call(
        _swa_attn_kernel,
        grid_spec=pltpu.PrefetchScalarGridSpec(
            num_scalar_prefetch=1,
            grid=(batch, nt),
            in_specs=[cur(D_MODEL), prev(SWA_KV), cur(SWA_KV), prev(SWA_KV), cur(SWA_KV)],
            out_specs=cur(D_MODEL)),
        out_shape=jax.ShapeDtypeStruct((batch * seq, D_MODEL), BF16),
        compiler_params=pltpu.CompilerParams(
            dimension_semantics=("parallel", "parallel"), vmem_limit_bytes=VMEM_LIMIT),
        name="swa_attn",
    )(sinks, q, k, k, v, v)


def _post_mixer_kernel(*refs, has_out_bias, next_qkv):
    it = iter(refs)
    mix_ref, h_ref, p_ref, wo_ref = next(it), next(it), next(it), next(it)
    bo_ref = next(it) if has_out_bias else None
    g1_ref, b1_ref, wup_ref, wdn_ref, g2_ref, b2_ref, wg_ref, bg_ref, wp_ref = (
        next(it) for _ in range(9))
    wqkv_ref, bqkv_ref = (next(it), next(it)) if next_qkv else (None, None)
    out_ref = next(it)
    q_ref, k_ref, v_ref = (next(it), next(it), next(it)) if next_qkv else (None, None, None)

    half = POST_TILE // 2
    rows = [slice(0, half), slice(half, POST_TILE)]

    def out_proj(r):
        y = DEEPNORM_ALPHA * h_ref[r, :] + _dot(mix_ref[r, :], wo_ref[...])
        return y + bo_ref[...] if has_out_bias else y

    def mlp(h1b, acc, chunks):
        for c in chunks:
            u = jnp.maximum(_dot(h1b, wup_ref[:, c * FF_TILE:(c + 1) * FF_TILE]), 0.0)
            acc = acc + _dot((u * u).astype(BF16), wdn_ref[c * FF_TILE:(c + 1) * FF_TILE, :])
        return acc

    def ple(r, h2):
        z = _dot(h2.astype(BF16), wg_ref[...]) + bg_ref[...]
        gate = 1.0 / (1.0 + jnp.exp(-z))
        out = h2 + gate * _dot(p_ref[r, :].astype(BF16), wp_ref[...])
        out_ref[r, :] = out
        if next_qkv:
            ob = out.astype(BF16)
            kcol, vcol = D_MODEL, D_MODEL + SWA_KV
            q = _dot(ob, wqkv_ref[:, :kcol]) + bqkv_ref[:, :kcol]
            q_ref[r, :] = (q * (SWA_HEAD_DIM ** -0.5)).astype(BF16)
            k_ref[r, :] = (_dot(ob, wqkv_ref[:, kcol:vcol]) + bqkv_ref[:, kcol:vcol]).astype(BF16)
            v_ref[r, :] = (_dot(ob, wqkv_ref[:, vcol:]) + bqkv_ref[:, vcol:]).astype(BF16)

    n_ff = D_FF // FF_TILE
    y_a = out_proj(rows[0])
    y_b = out_proj(rows[1])
    h1_a = _layer_norm(y_a, g1_ref[...], b1_ref[...])
    h1b_a = h1_a.astype(BF16)
    acc_a = mlp(h1b_a, DEEPNORM_ALPHA * h1_a, range(0, 1))
    h1_b = _layer_norm(y_b, g1_ref[...], b1_ref[...])
    h1b_b = h1_b.astype(BF16)
    acc_a = mlp(h1b_a, acc_a, range(1, n_ff))
    acc_b = mlp(h1b_b, DEEPNORM_ALPHA * h1_b, range(0, 1))
    h2_a = _layer_norm(acc_a, g2_ref[...], b2_ref[...])
    acc_b = mlp(h1b_b, acc_b, range(1, n_ff))
    h2_b = _layer_norm(acc_b, g2_ref[...], b2_ref[...])
    ple(rows[0], h2_a)
    ple(rows[1], h2_b)


def _post_mixer(mix, h2d, p3, layer, weights, out_bias=None, qkv=None):
    t = h2d.shape[0]
    row = lambda w: pl.BlockSpec((POST_TILE, w), lambda i: (i, 0))
    wo, rest = weights[0], tuple(weights[1:])
    consts = (wo,) + ((out_bias,) if out_bias is not None else ()) + rest + (qkv or ())
    out_specs = [row(D_MODEL)]
    out_shape = [jax.ShapeDtypeStruct((t, D_MODEL), F32)]
    if qkv is not None:
        out_specs += [row(D_MODEL), row(SWA_KV), row(SWA_KV)]
        out_shape += [jax.ShapeDtypeStruct((t, D_MODEL), BF16),
                      jax.ShapeDtypeStruct((t, SWA_KV), BF16),
                      jax.ShapeDtypeStruct((t, SWA_KV), BF16)]
    return pl.pallas_call(
        functools.partial(_post_mixer_kernel, has_out_bias=out_bias is not None,
                          next_qkv=qkv is not None),
        grid=(t // POST_TILE,),
        in_specs=[row(D_MODEL), row(D_MODEL),
                  pl.BlockSpec((None, POST_TILE, PLE_DIM), lambda i: (layer, i, 0))]
                 + [_resident(w.shape, layer if w.ndim == 3 else None) for w in consts],
        out_specs=out_specs,
        out_shape=out_shape,
        compiler_params=pltpu.CompilerParams(
            dimension_semantics=("parallel",), vmem_limit_bytes=VMEM_LIMIT),
        name=f"post_mixer_{layer}",
    )(mix, h2d, p3, *consts)


def kernel(x, p, gla_w_in, gla_w_gk_up, gla_b_gk, gla_norm_g, gla_w_out, swa_w_qkv, swa_b_qkv,
           swa_sinks, swa_w_out, swa_b_out, mlp_w_up, mlp_w_down, ln1_g, ln1_b, ln2_g, ln2_b,
           ple_w_proj, ple_w_gate, ple_b_gate):
    batch, seq, d = x.shape
    t = batch * seq
    x2 = x.reshape(t, d)
    p3 = p.reshape(DEPTH, t, PLE_DIM)
    vec = lambda a: a.reshape(1, -1).astype(F32)

    w_up, w_down = mlp_w_up.astype(BF16), mlp_w_down.astype(BF16)
    w_gate, w_proj = ple_w_gate.astype(BF16), ple_w_proj.astype(BF16)

    def post_weights(layer, wo):
        return (wo.astype(BF16), vec(ln1_g[layer]), vec(ln1_b[layer]), w_up, w_down,
                vec(ln2_g[layer]), vec(ln2_b[layer]), w_gate, vec(ple_b_gate[layer]), w_proj)

    norm_g = vec(jnp.tile(gla_norm_g[0], GLA_HEADS))
    mix = _gla(x2, gla_w_in[0].astype(BF16), gla_w_gk_up[0].astype(BF16), vec(gla_b_gk[0]),
               norm_g, seq)
    h, q, k, v = _post_mixer(mix, x2, p3, 0, post_weights(0, gla_w_out[0]),
                             qkv=(swa_w_qkv[0].astype(BF16), vec(swa_b_qkv[0])))

    mix = _swa_attn(swa_sinks[0].astype(F32), q, k, v, batch, seq)
    (h,) = _post_mixer(mix, h, p3, 1, post_weights(1, swa_w_out[0]), out_bias=vec(swa_b_out[0]))
    return h.reshape(batch, seq, d)
```

```python
import functools

import jax
import jax.numpy as jnp
from jax import lax
from jax.experimental import pallas as pl
from jax.experimental.pallas import tpu as pltpu

D_MODEL = 1024
DEPTH = 2
PLE_DIM = 256

GLA_HEADS = 4
GLA_DK = 128
GLA_DV = 256
GLA_LOWRANK = 16
GLA_TAU = 16.0
GLA_QK = GLA_HEADS * GLA_DK
GLA_V = GLA_HEADS * GLA_DV

SWA_HEAD_DIM = 64
SWA_Q_HEADS = 16
SWA_KV_HEADS = 4
SWA_GROUP = SWA_Q_HEADS // SWA_KV_HEADS
SWA_BLOCK = 128
SWA_Q_TILE = 1024
SWA_KV = SWA_KV_HEADS * SWA_HEAD_DIM

D_FF = 4 * D_MODEL
DEEPNORM_ALPHA = (2.0 * DEPTH) ** 0.25
LN_EPS = 1e-5
RMS_EPS = 1e-5

LANES = 128
BF16_SUBLANES = 16
GLA_CHUNK = 128
ROW_TILE = 512
POST_TILE = 512
FF_TILE = 1024
VMEM_LIMIT = 60 * 1024 * 1024

BF16 = jnp.bfloat16
F32 = jnp.float32


def _dot(a, b):
    return jnp.dot(a, b, preferred_element_type=F32)


def _dot_nt(a, b):
    return lax.dot_general(a, b, (((1,), (1,)), ((), ())), preferred_element_type=F32)


def _resident(shape, layer=None):
    if layer is None:
        return pl.BlockSpec(shape, lambda *_: (0, 0), pipeline_mode=pl.Buffered(1))
    return pl.BlockSpec((None,) + tuple(shape[1:]), lambda *_: (layer, 0, 0),
                        pipeline_mode=pl.Buffered(1))


def _layer_norm(y, g, b):
    mu = jnp.mean(y, axis=-1, keepdims=True)
    yc = y - mu
    var = jnp.mean(yc * yc, axis=-1, keepdims=True)
    return yc * lax.rsqrt(var + LN_EPS) * g + b


def _gla_kernel(*refs, n_cast, tiles_per_seq):
    x_ref, w_ref, wup_ref, bgk_ref, ng_ref = refs[:5]
    cast_in = refs[5:5 + n_cast]
    o_ref = refs[5 + n_cast]
    cast_out = refs[6 + n_cast:6 + 2 * n_cast]
    q_s, k_s, v_s, g_s, b_s, state_ref = refs[6 + 2 * n_cast:]
    i = pl.program_id(0)

    @pl.when(i == 0)
    def _():
        q_s[1] = jnp.zeros(q_s.shape[1:], q_s.dtype)
        k_s[1] = jnp.zeros(k_s.shape[1:], k_s.dtype)
        v_s[1] = jnp.zeros(v_s.shape[1:], v_s.dtype)
        g_s[1] = jnp.zeros(g_s.shape[1:], g_s.dtype)
        b_s[1] = jnp.zeros(b_s.shape[1:], b_s.dtype)
        state_ref[...] = jnp.zeros_like(state_ref)

    for parity in range(2):
        @pl.when(i % 2 == parity)
        def _():
            _gla_step(x_ref, w_ref, wup_ref, bgk_ref, ng_ref, o_ref, q_s, k_s, v_s, g_s, b_s,
                      state_ref, cast_in, cast_out, wr=parity, tiles_per_seq=tiles_per_seq)


def _gla_step(x_ref, w_ref, wup_ref, bgk_ref, ng_ref, o_ref,
              q_s, k_s, v_s, g_s, b_s, state_ref, cast_in, cast_out, *, wr, tiles_per_seq):
    i = pl.program_id(0)
    rd = 1 - wr
    c_k, c_v, c_r, c_gl = GLA_QK, 2 * GLA_QK, 2 * GLA_QK + GLA_V, 2 * GLA_QK + 2 * GLA_V
    row = lax.broadcasted_iota(jnp.int32, (GLA_CHUNK, GLA_CHUNK), 0)
    col = lax.broadcasted_iota(jnp.int32, (GLA_CHUNK, GLA_CHUNK), 1)
    causal = col <= row
    tril2 = jnp.concatenate([causal.astype(BF16)] * 2, axis=1)
    heads = range(GLA_HEADS)
    ks = [slice(h * GLA_DK, (h + 1) * GLA_DK) for h in heads]
    vs = [slice(h * GLA_DV, (h + 1) * GLA_DV) for h in heads]

    fresh = (i - 1) % tiles_per_seq == 0
    state = [jnp.where(fresh, 0.0, state_ref[h]) for h in heads]

    n_chunks = ROW_TILE // GLA_CHUNK
    chunk_rows = [slice(c * GLA_CHUNK, (c + 1) * GLA_CHUNK) for c in range(n_chunks)]

    def prepare(c):
        out = []
        for h in heads:
            b = b_s[rd, chunk_rows[c], ks[h]]
            b_last = b[GLA_CHUNK - 1:GLA_CHUNK, :]
            half = 0.5 * b_last
            e_half = jnp.exp(half).astype(BF16)
            bq = b - half
            qd = q_s[rd, chunk_rows[c], ks[h]] * jnp.exp(bq).astype(BF16)
            kd = k_s[rd, chunk_rows[c], ks[h]] * jnp.exp(-bq).astype(BF16)
            dec = jnp.broadcast_to(jnp.exp(b_last), (GLA_CHUNK, GLA_DK)).T
            out.append(dict(att=_dot_nt(qd, kd), q_in=qd * e_half,
                            kl=kd * e_half, dec=dec))
        return out

    half_v = GLA_V // 2
    xb = x_ref[...].astype(BF16)
    gl = _dot(xb, w_ref[:, c_gl:]).astype(BF16)
    prep = [prepare(0)]
    r0 = _dot(xb, w_ref[:, c_r:c_r + half_v])
    prep.append(prepare(1))
    r1 = _dot(xb, w_ref[:, c_r + half_v:c_gl])
    z = _dot(gl, wup_ref[...]) + bgk_ref[...]
    prep.append(prepare(2))
    q_s[wr] = (_dot(xb, w_ref[:, :c_k]) * (GLA_DK ** -0.5)).astype(BF16)
    prep.append(prepare(3))
    k_s[wr] = _dot(xb, w_ref[:, c_k:c_v]).astype(BF16)
    for src, dst in zip(cast_in, cast_out):
        dst[...] = src[...].astype(BF16)

    upd = [[lax.dot_general(prep[c][h]["kl"], v_s[rd, chunk_rows[c], vs[h]],
                            (((0,), (0,)), ((), ())), preferred_element_type=F32)
            for h in heads] for c in range(n_chunks)]
    r = jnp.concatenate([r0, r1], axis=1)
    g_s[wr] = (r / (1.0 + jnp.exp(-r)) * ng_ref[...]).astype(BF16)
    states = [state]
    for c in range(n_chunks):
        states.append([jnp.concatenate(
            [states[c][h][:, j * GLA_CHUNK:(j + 1) * GLA_CHUNK] * prep[c][h]["dec"]
             for j in range(GLA_DV // GLA_CHUNK)], axis=1) + upd[c][h] for h in heads])
    v_s[wr, :, :half_v] = _dot(xb, w_ref[:, c_v:c_v + half_v]).astype(BF16)

    o = []
    for c in range(n_chunks):
        o.append([_dot(
            jnp.concatenate([jnp.where(causal, prep[c][h]["att"], 0.0).astype(BF16),
                             prep[c][h]["q_in"]], axis=1),
            jnp.concatenate([v_s[rd, chunk_rows[c], vs[h]], states[c][h].astype(BF16)], axis=0))
            for h in heads])
        if c == 1:
            v_s[wr, :, half_v:] = _dot(xb, w_ref[:, c_v + half_v:c_r]).astype(BF16)
    log_a = (jnp.minimum(z, 0.0) - jnp.log1p(jnp.exp(-jnp.abs(z)))) * (1.0 / GLA_TAU)
    hi = log_a.astype(BF16)
    lo = (log_a - hi.astype(F32)).astype(BF16)
    for c in range(n_chunks):
        b_s[wr, chunk_rows[c], :] = _dot(
            tril2, jnp.concatenate([hi[chunk_rows[c]], lo[chunk_rows[c]]], axis=0))
    for c in range(n_chunks):
        for h in heads:
            inv_rms = lax.rsqrt(jnp.mean(o[c][h] * o[c][h], axis=-1, keepdims=True) + RMS_EPS)
            gate = g_s[rd, chunk_rows[c], vs[h]].astype(F32)
            o_ref[chunk_rows[c], vs[h]] = (o[c][h] * inv_rms * gate).astype(BF16)
    state = states[n_chunks]
    for h in heads:
        state_ref[h] = state[h]


def _gla(x2, w_in, wup, bgk, ng, seq, cast):
    t = x2.shape[0]
    n = t // ROW_TILE
    assert ROW_TILE // GLA_CHUNK == 4 and seq % ROW_TILE == 0

    def cast_spec(w):
        rows = max(BF16_SUBLANES, w.shape[0] // n)
        last = w.shape[0] // rows - 1
        return pl.BlockSpec((rows, w.shape[1]), lambda i: (jnp.minimum(i, last), 0))

    cast_specs = [cast_spec(w) for w in cast]
    return pl.pallas_call(
        functools.partial(_gla_kernel, n_cast=len(cast), tiles_per_seq=seq // ROW_TILE),
        grid=(n + 1,),
        in_specs=[pl.BlockSpec((ROW_TILE, D_MODEL), lambda i: (jnp.minimum(i, n - 1), 0)),
                  _resident(w_in.shape), _resident(wup.shape), _resident(bgk.shape),
                  _resident(ng.shape)] + cast_specs,
        out_specs=[pl.BlockSpec((ROW_TILE, GLA_V), lambda i: (jnp.maximum(i - 1, 0), 0))]
                  + cast_specs,
        out_shape=[jax.ShapeDtypeStruct((t, GLA_V), BF16)]
                  + [jax.ShapeDtypeStruct(w.shape, BF16) for w in cast],
        scratch_shapes=[pltpu.VMEM((2, ROW_TILE, GLA_QK), BF16),
                        pltpu.VMEM((2, ROW_TILE, GLA_QK), BF16),
                        pltpu.VMEM((2, ROW_TILE, GLA_V), BF16),
                        pltpu.VMEM((2, ROW_TILE, GLA_V), BF16),
                        pltpu.VMEM((2, ROW_TILE, GLA_QK), F32),
                        pltpu.VMEM((GLA_HEADS, GLA_DK, GLA_DV), F32)],
        compiler_params=pltpu.CompilerParams(
            dimension_semantics=("arbitrary",), vmem_limit_bytes=VMEM_LIMIT),
        name="gla",
    )(x2, w_in, wup, bgk, ng, *cast)


def _swa_attn_kernel(sink_ref, q_ref, kp_ref, kc_ref, vp_ref, vc_ref, o_ref):
    has_prev = (pl.program_id(1) > 0).astype(BF16)
    k_all = jnp.concatenate([kp_ref[...] * has_prev, kc_ref[...]], axis=0)
    v_all = jnp.concatenate([vp_ref[...] * has_prev, vc_ref[...]], axis=0)
    grp = SWA_GROUP * SWA_BLOCK
    kj = lax.broadcasted_iota(jnp.int32, (SWA_BLOCK, grp), 0)
    qi = lax.broadcasted_iota(jnp.int32, (SWA_BLOCK, grp), 1) % SWA_BLOCK
    from_prev = kj > qi
    lane = lax.broadcasted_iota(jnp.int32, (SWA_BLOCK, LANES), 1)
    even_lanes = lane < SWA_HEAD_DIM
    one = jnp.ones((2 * SWA_BLOCK, SWA_HEAD_DIM), BF16)
    pairs_per_kv = SWA_GROUP // 2
    sink_rows = [jnp.concatenate(
        [jnp.full((1, SWA_BLOCK), sink_ref[kv * SWA_GROUP + g], F32) for g in range(SWA_GROUP)],
        axis=1) for kv in range(SWA_KV_HEADS)]
    units = [(blk, kv) for blk in range(SWA_Q_TILE // SWA_BLOCK) for kv in range(SWA_KV_HEADS)]
    band = lambda blk: slice(blk * SWA_BLOCK, (blk + 2) * SWA_BLOCK)
    rows = lambda blk: slice(blk * SWA_BLOCK, (blk + 1) * SWA_BLOCK)
    heads = lambda kv: slice(kv * SWA_HEAD_DIM, (kv + 1) * SWA_HEAD_DIM)
    scores, maxes = [], []
    for blk, kv in units:
        kh = k_all[band(blk), heads(kv)]
        k2 = jnp.concatenate([kh, kh], axis=1)
        q_heads = []
        for j in range(pairs_per_kv):
            hp = kv * pairs_per_kv + j
            q_pair = q_ref[rows(blk), hp * LANES:(hp + 1) * LANES]
            q_heads += [jnp.where(even_lanes, q_pair, jnp.zeros_like(q_pair)),
                        jnp.where(even_lanes, jnp.zeros_like(q_pair), q_pair)]
        s2 = _dot_nt(k2, jnp.concatenate(q_heads, axis=0))
        s = jnp.where(from_prev, s2[:SWA_BLOCK], s2[SWA_BLOCK:])
        scores.append(s)
        maxes.append(jnp.maximum(jnp.max(s, axis=0, keepdims=True), sink_rows[kv]))
    probs = []
    for u in range(len(units)):
        p = jnp.exp(scores[u] - maxes[u])
        probs.append(jnp.concatenate([jnp.where(from_prev, p, 0.0),
                                      jnp.where(from_prev, 0.0, p)], axis=0).astype(BF16))
    outs = []
    for u, (blk, kv) in enumerate(units):
        v_ext = jnp.concatenate([v_all[band(blk), heads(kv)], one], axis=1)
        pv = lax.dot_general(v_ext, probs[u], (((0,), (0,)), ((), ())),
                             preferred_element_type=F32)
        den = pv[SWA_HEAD_DIM:] + jnp.exp(sink_rows[kv] - maxes[u])
        outs.append(pv[:SWA_HEAD_DIM] / den)
    for u, (blk, kv) in enumerate(units):
        for j in range(pairs_per_kv):
            hp = kv * pairs_per_kv + j
            g0 = j * 2 * SWA_BLOCK
            pair = jnp.concatenate([outs[u][:, g0:g0 + SWA_BLOCK],
                                    outs[u][:, g0 + SWA_BLOCK:g0 + 2 * SWA_BLOCK]], axis=0)
            o_ref[rows(blk), hp * LANES:(hp + 1) * LANES] = pair.T.astype(BF16)


def _swa_attn(sinks, q, k, v, batch, seq):
    nt = seq // SWA_Q_TILE
    per = SWA_Q_TILE // SWA_BLOCK
    cur = lambda w: pl.BlockSpec((SWA_Q_TILE, w), lambda b, n, *_: (b * nt + n, 0))
    prev = lambda w: pl.BlockSpec(
        (SWA_BLOCK, w), lambda b, n, *_: (b * nt * per + jnp.maximum(n * per - 1, 0), 0))
    return pl.pallas_call(
        _swa_attn_kernel,
        grid_spec=pltpu.PrefetchScalarGridSpec(
            num_scalar_prefetch=1,
            grid=(batch, nt),
            in_specs=[cur(D_MODEL), prev(SWA_KV), cur(SWA_KV), prev(SWA_KV), cur(SWA_KV)],
            out_specs=cur(D_MODEL)),
        out_shape=jax.ShapeDtypeStruct((batch * seq, D_MODEL), BF16),
        compiler_params=pltpu.CompilerParams(
            dimension_semantics=("parallel", "parallel"), vmem_limit_bytes=VMEM_LIMIT),
        name="swa_attn",
    )(sinks, q, k, k, v, v)


def _post_mixer_kernel(*refs, has_out_bias, next_qkv):
    it = iter(refs)
    mix_ref, h_ref, p_ref, wo_ref = next(it), next(it), next(it), next(it)
    bo_ref = next(it) if has_out_bias else None
    g1_ref, b1_ref, wup_ref, wdn_ref, g2_ref, b2_ref, wg_ref, bg_ref, wp_ref = (
        next(it) for _ in range(9))
    wqkv_ref, bqkv_ref = (next(it), next(it)) if next_qkv else (None, None)
    out_ref = next(it)
    q_ref, k_ref, v_ref = (next(it), next(it), next(it)) if next_qkv else (None, None, None)

    half = POST_TILE // 2
    rows = [slice(0, half), slice(half, POST_TILE)]

    def out_proj(r):
        y = DEEPNORM_ALPHA * h_ref[r, :] + _dot(mix_ref[r, :], wo_ref[...])
        return y + bo_ref[...] if has_out_bias else y

    def mlp(h1b, acc, chunks):
        for c in chunks:
            u = jnp.maximum(_dot(h1b, wup_ref[:, c * FF_TILE:(c + 1) * FF_TILE]), 0.0)
            acc = acc + _dot((u * u).astype(BF16), wdn_ref[c * FF_TILE:(c + 1) * FF_TILE, :])
        return acc

    def ple(r, h2):
        z = _dot(h2.astype(BF16), wg_ref[...]) + bg_ref[...]
        gate = 1.0 / (1.0 + jnp.exp(-z))
        out = h2 + gate * _dot(p_ref[r, :].astype(BF16), wp_ref[...])
        out_ref[r, :] = out
        if next_qkv:
            ob = out.astype(BF16)
            kcol, vcol = D_MODEL, D_MODEL + SWA_KV
            q = _dot(ob, wqkv_ref[:, :kcol]) + bqkv_ref[:, :kcol]
            q_ref[r, :] = (q * (SWA_HEAD_DIM ** -0.5)).astype(BF16)
            k_ref[r, :] = (_dot(ob, wqkv_ref[:, kcol:vcol]) + bqkv_ref[:, kcol:vcol]).astype(BF16)
            v_ref[r, :] = (_dot(ob, wqkv_ref[:, vcol:]) + bqkv_ref[:, vcol:]).astype(BF16)

    n_ff = D_FF // FF_TILE
    y_a = out_proj(rows[0])
    y_b = out_proj(rows[1])
    h1_a = _layer_norm(y_a, g1_ref[...], b1_ref[...])
    h1b_a = h1_a.astype(BF16)
    acc_a = mlp(h1b_a, DEEPNORM_ALPHA * h1_a, range(0, 1))
    h1_b = _layer_norm(y_b, g1_ref[...], b1_ref[...])
    h1b_b = h1_b.astype(BF16)
    acc_a = mlp(h1b_a, acc_a, range(1, n_ff))
    acc_b = mlp(h1b_b, DEEPNORM_ALPHA * h1_b, range(0, 1))
    h2_a = _layer_norm(acc_a, g2_ref[...], b2_ref[...])
    acc_b = mlp(h1b_b, acc_b, range(1, n_ff))
    h2_b = _layer_norm(acc_b, g2_ref[...], b2_ref[...])
    ple(rows[0], h2_a)
    ple(rows[1], h2_b)


def _post_mixer(mix, h2d, p3, layer, weights, out_bias=None, qkv=None):
    t = h2d.shape[0]
    row = lambda w: pl.BlockSpec((POST_TILE, w), lambda i: (i, 0))
    wo, rest = weights[0], tuple(weights[1:])
    consts = (wo,) + ((out_bias,) if out_bias is not None else ()) + rest + (qkv or ())
    out_specs = [row(D_MODEL)]
    out_shape = [jax.ShapeDtypeStruct((t, D_MODEL), F32)]
    if qkv is not None:
        out_specs += [row(D_MODEL), row(SWA_KV), row(SWA_KV)]
        out_shape += [jax.ShapeDtypeStruct((t, D_MODEL), BF16),
                      jax.ShapeDtypeStruct((t, SWA_KV), BF16),
                      jax.ShapeDtypeStruct((t, SWA_KV), BF16)]
    return pl.pallas_call(
        functools.partial(_post_mixer_kernel, has_out_bias=out_bias is not None,
                          next_qkv=qkv is not None),
        grid=(t // POST_TILE,),
        in_specs=[row(D_MODEL), row(D_MODEL),
                  pl.BlockSpec((None, POST_TILE, PLE_DIM), lambda i: (layer, i, 0))]
                 + [_resident(w.shape, layer if w.ndim == 3 else None) for w in consts],
        out_specs=out_specs,
        out_shape=out_shape,
        compiler_params=pltpu.CompilerParams(
            dimension_semantics=("parallel",), vmem_limit_bytes=VMEM_LIMIT),
        name=f"post_mixer_{layer}",
    )(mix, h2d, p3, *consts)


def kernel(x, p, gla_w_in, gla_w_gk_up, gla_b_gk, gla_norm_g, gla_w_out, swa_w_qkv, swa_b_qkv,
           swa_sinks, swa_w_out, swa_b_out, mlp_w_up, mlp_w_down, ln1_g, ln1_b, ln2_g, ln2_b,
           ple_w_proj, ple_w_gate, ple_b_gate):
    batch, seq, d = x.shape
    t = batch * seq
    x2 = x.reshape(t, d)
    p3 = p.reshape(DEPTH, t, PLE_DIM)
    vec = lambda a: a.reshape(1, -1).astype(F32)

    flat = lambda w: w.reshape(-1, w.shape[-1])
    norm_g = vec(jnp.tile(gla_norm_g[0], GLA_HEADS))
    mix, w_up, w_down, w_gate, w_proj, w_out0, w_out1, w_qkv = _gla(
        x2, gla_w_in[0].astype(BF16), gla_w_gk_up[0].astype(BF16), vec(gla_b_gk[0]), norm_g, seq,
        cast=[flat(mlp_w_up), flat(mlp_w_down), flat(ple_w_gate), flat(ple_w_proj),
              gla_w_out[0], swa_w_out[0], swa_w_qkv[0]])
    w_up, w_down = w_up.reshape(mlp_w_up.shape), w_down.reshape(mlp_w_down.shape)
    w_gate, w_proj = w_gate.reshape(ple_w_gate.shape), w_proj.reshape(ple_w_proj.shape)

    def post_weights(layer, wo):
        return (wo, vec(ln1_g[layer]), vec(ln1_b[layer]), w_up, w_down,
                vec(ln2_g[layer]), vec(ln2_b[layer]), w_gate, vec(ple_b_gate[layer]), w_proj)

    h, q, k, v = _post_mixer(mix, x2, p3, 0, post_weights(0, w_out0),
                             qkv=(w_qkv, vec(swa_b_qkv[0])))

    mix = _swa_attn(swa_sinks[0].astype(F32), q, k, v, batch, seq)
    (h,) = _post_mixer(mix, h, p3, 1, post_weights(1, w_out1), out_bias=vec(swa_b_out[0]))
    return h.reshape(batch, seq, d)
```

```python
import functools

import jax
import jax.numpy as jnp
from jax import lax
from jax.experimental import pallas as pl
from jax.experimental.pallas import tpu as pltpu

D_MODEL = 1024
DEPTH = 2
PLE_DIM = 256

GLA_HEADS = 4
GLA_DK = 128
GLA_DV = 256
GLA_LOWRANK = 16
GLA_TAU = 16.0
GLA_QK = GLA_HEADS * GLA_DK
GLA_V = GLA_HEADS * GLA_DV

SWA_HEAD_DIM = 64
SWA_Q_HEADS = 16
SWA_KV_HEADS = 4
SWA_GROUP = SWA_Q_HEADS // SWA_KV_HEADS
SWA_BLOCK = 128
SWA_Q_TILE = 1024
SWA_KV = SWA_KV_HEADS * SWA_HEAD_DIM

D_FF = 4 * D_MODEL
DEEPNORM_ALPHA = (2.0 * DEPTH) ** 0.25
LN_EPS = 1e-5
RMS_EPS = 1e-5
LOG2E = 1.4426950408889634

LANES = 128
BF16_SUBLANES = 16
GLA_CHUNK = 128
ROW_TILE = 512
POST_TILE = 512
FF_TILE = 1024
VMEM_LIMIT = 60 * 1024 * 1024

BF16 = jnp.bfloat16
F32 = jnp.float32


def _dot(a, b):
    return jnp.dot(a, b, preferred_element_type=F32)


def _dot_nt(a, b):
    return lax.dot_general(a, b, (((1,), (1,)), ((), ())), preferred_element_type=F32)


def _resident(shape, layer=None):
    if layer is None:
        return pl.BlockSpec(shape, lambda *_: (0, 0), pipeline_mode=pl.Buffered(1))
    return pl.BlockSpec((None,) + tuple(shape[1:]), lambda *_: (layer, 0, 0),
                        pipeline_mode=pl.Buffered(1))


def _layer_norm(y, g, b):
    mu = jnp.mean(y, axis=-1, keepdims=True)
    yc = y - mu
    var = jnp.mean(yc * yc, axis=-1, keepdims=True)
    return yc * lax.rsqrt(var + LN_EPS) * g + b


def _gla_kernel(*refs, n_cast, tiles_per_seq):
    x_ref, w_ref, wup_ref, bgk_ref, ng_ref = refs[:5]
    cast_in = refs[5:5 + n_cast]
    o_ref = refs[5 + n_cast]
    cast_out = refs[6 + n_cast:6 + 2 * n_cast]
    q_s, k_s, v_s, g_s, b_s, state_ref = refs[6 + 2 * n_cast:]
    i = pl.program_id(0)

    @pl.when(i == 0)
    def _():
        q_s[1] = jnp.zeros(q_s.shape[1:], q_s.dtype)
        k_s[1] = jnp.zeros(k_s.shape[1:], k_s.dtype)
        v_s[1] = jnp.zeros(v_s.shape[1:], v_s.dtype)
        g_s[1] = jnp.zeros(g_s.shape[1:], g_s.dtype)
        b_s[1] = jnp.zeros(b_s.shape[1:], b_s.dtype)
        state_ref[...] = jnp.zeros_like(state_ref)

    for parity in range(2):
        @pl.when(i % 2 == parity)
        def _():
            _gla_step(x_ref, w_ref, wup_ref, bgk_ref, ng_ref, o_ref, q_s, k_s, v_s, g_s, b_s,
                      state_ref, cast_in, cast_out, wr=parity, tiles_per_seq=tiles_per_seq)


def _gla_step(x_ref, w_ref, wup_ref, bgk_ref, ng_ref, o_ref,
              q_s, k_s, v_s, g_s, b_s, state_ref, cast_in, cast_out, *, wr, tiles_per_seq):
    i = pl.program_id(0)
    rd = 1 - wr
    c_k, c_v, c_r, c_gl = GLA_QK, 2 * GLA_QK, 2 * GLA_QK + GLA_V, 2 * GLA_QK + 2 * GLA_V
    row = lax.broadcasted_iota(jnp.int32, (GLA_CHUNK, GLA_CHUNK), 0)
    col = lax.broadcasted_iota(jnp.int32, (GLA_CHUNK, GLA_CHUNK), 1)
    causal = col <= row
    tril2 = jnp.concatenate([causal.astype(BF16)] * 2, axis=1)
    heads = range(GLA_HEADS)
    ks = [slice(h * GLA_DK, (h + 1) * GLA_DK) for h in heads]
    vs = [slice(h * GLA_DV, (h + 1) * GLA_DV) for h in heads]

    fresh = (i - 1) % tiles_per_seq == 0
    state = [jnp.where(fresh, 0.0, state_ref[h]) for h in heads]

    n_chunks = ROW_TILE // GLA_CHUNK
    chunk_rows = [slice(c * GLA_CHUNK, (c + 1) * GLA_CHUNK) for c in range(n_chunks)]

    def prepare(c):
        out = []
        for h in heads:
            b = b_s[rd, chunk_rows[c], ks[h]]
            b_last = b[GLA_CHUNK - 1:GLA_CHUNK, :]
            half = 0.5 * b_last
            e_half = jnp.exp(half).astype(BF16)
            bq = b - half
            qd = q_s[rd, chunk_rows[c], ks[h]] * jnp.exp(bq).astype(BF16)
            kd = k_s[rd, chunk_rows[c], ks[h]] * jnp.exp(-bq).astype(BF16)
            dec = jnp.broadcast_to(jnp.exp(b_last), (GLA_CHUNK, GLA_DK)).T
            out.append(dict(att=_dot_nt(qd, kd), q_in=qd * e_half,
                            kl=kd * e_half, dec=dec))
        return out

    half_v = GLA_V // 2
    xb = x_ref[...].astype(BF16)
    gl = _dot(xb, w_ref[:, c_gl:]).astype(BF16)
    prep = [prepare(0)]
    r0 = _dot(xb, w_ref[:, c_r:c_r + half_v])
    prep.append(prepare(1))
    r1 = _dot(xb, w_ref[:, c_r + half_v:c_gl])
    z = _dot(gl, wup_ref[...].astype(BF16)) + bgk_ref[...]
    prep.append(prepare(2))
    q_s[wr] = (_dot(xb, w_ref[:, :c_k]) * (GLA_DK ** -0.5)).astype(BF16)
    prep.append(prepare(3))
    k_s[wr] = _dot(xb, w_ref[:, c_k:c_v]).astype(BF16)
    for src, dst in zip(cast_in, cast_out):
        dst[...] = src[...].astype(BF16)

    upd = [[lax.dot_general(prep[c][h]["kl"], v_s[rd, chunk_rows[c], vs[h]],
                            (((0,), (0,)), ((), ())), preferred_element_type=F32)
            for h in heads] for c in range(n_chunks)]
    r = jnp.concatenate([r0, r1], axis=1)
    g_s[wr] = (r / (1.0 + jnp.exp(-r)) * ng_ref[...]).astype(BF16)
    states = [state]
    for c in range(n_chunks):
        states.append([jnp.concatenate(
            [states[c][h][:, j * GLA_CHUNK:(j + 1) * GLA_CHUNK] * prep[c][h]["dec"]
             for j in range(GLA_DV // GLA_CHUNK)], axis=1) + upd[c][h] for h in heads])
    v_s[wr, :, :half_v] = _dot(xb, w_ref[:, c_v:c_v + half_v]).astype(BF16)

    o = []
    for c in range(n_chunks):
        o.append([_dot(
            jnp.concatenate([jnp.where(causal, prep[c][h]["att"], 0.0).astype(BF16),
                             prep[c][h]["q_in"]], axis=1),
            jnp.concatenate([v_s[rd, chunk_rows[c], vs[h]], states[c][h].astype(BF16)], axis=0))
            for h in heads])
        if c == 1:
            v_s[wr, :, half_v:] = _dot(xb, w_ref[:, c_v + half_v:c_r]).astype(BF16)
    log_a = (jnp.minimum(z, 0.0) - jnp.log1p(jnp.exp(-jnp.abs(z)))) * (1.0 / GLA_TAU)
    hi = log_a.astype(BF16)
    lo = (log_a - hi.astype(F32)).astype(BF16)
    for c in range(n_chunks):
        b_s[wr, chunk_rows[c], :] = _dot(
            tril2, jnp.concatenate([hi[chunk_rows[c]], lo[chunk_rows[c]]], axis=0))
    for c in range(n_chunks):
        for h in heads:
            inv_rms = lax.rsqrt(jnp.mean(o[c][h] * o[c][h], axis=-1, keepdims=True) + RMS_EPS)
            gate = g_s[rd, chunk_rows[c], vs[h]].astype(F32)
            o_ref[chunk_rows[c], vs[h]] = (o[c][h] * inv_rms * gate).astype(BF16)
    state = states[n_chunks]
    for h in heads:
        state_ref[h] = state[h]


def _gla(x2, w_in, wup, bgk, ng, seq, cast):
    t = x2.shape[0]
    n = t // ROW_TILE
    assert ROW_TILE // GLA_CHUNK == 4 and seq % ROW_TILE == 0

    def cast_spec(w):
        rows = max(BF16_SUBLANES, w.shape[0] // n)
        last = w.shape[0] // rows - 1
        return pl.BlockSpec((rows, w.shape[1]), lambda i: (jnp.minimum(i, last), 0))

    cast_specs = [cast_spec(w) for w in cast]
    return pl.pallas_call(
        functools.partial(_gla_kernel, n_cast=len(cast), tiles_per_seq=seq // ROW_TILE),
        grid=(n + 1,),
        in_specs=[pl.BlockSpec((ROW_TILE, D_MODEL), lambda i: (jnp.minimum(i, n - 1), 0)),
                  _resident(w_in.shape), _resident(wup.shape), _resident(bgk.shape),
                  _resident(ng.shape)] + cast_specs,
        out_specs=[pl.BlockSpec((ROW_TILE, GLA_V), lambda i: (jnp.maximum(i - 1, 0), 0))]
                  + cast_specs,
        out_shape=[jax.ShapeDtypeStruct((t, GLA_V), BF16)]
                  + [jax.ShapeDtypeStruct(w.shape, BF16) for w in cast],
        scratch_shapes=[pltpu.VMEM((2, ROW_TILE, GLA_QK), BF16),
                        pltpu.VMEM((2, ROW_TILE, GLA_QK), BF16),
                        pltpu.VMEM((2, ROW_TILE, GLA_V), BF16),
                        pltpu.VMEM((2, ROW_TILE, GLA_V), BF16),
                        pltpu.VMEM((2, ROW_TILE, GLA_QK), F32),
                        pltpu.VMEM((GLA_HEADS, GLA_DK, GLA_DV), F32)],
        compiler_params=pltpu.CompilerParams(
            dimension_semantics=("arbitrary",), vmem_limit_bytes=VMEM_LIMIT),
        name="gla",
    )(x2, w_in, wup, bgk, ng, *cast)


def _swa_attn_kernel(sink_ref, q_ref, kp_ref, kc_ref, vp_ref, vc_ref, o_ref):
    has_prev = (pl.program_id(1) > 0).astype(BF16)
    k_all = jnp.concatenate([kp_ref[...] * has_prev, kc_ref[...]], axis=0)
    v_all = jnp.concatenate([vp_ref[...] * has_prev, vc_ref[...]], axis=0)
    grp = SWA_GROUP * SWA_BLOCK
    kj = lax.broadcasted_iota(jnp.int32, (SWA_BLOCK, grp), 0)
    qi = lax.broadcasted_iota(jnp.int32, (SWA_BLOCK, grp), 1) % SWA_BLOCK
    from_prev = kj > qi
    lane = lax.broadcasted_iota(jnp.int32, (SWA_BLOCK, LANES), 1)
    even_lanes = lane < SWA_HEAD_DIM
    one = jnp.ones((2 * SWA_BLOCK, SWA_HEAD_DIM), BF16)
    pairs_per_kv = SWA_GROUP // 2
    sink_rows = [jnp.concatenate(
        [jnp.full((1, SWA_BLOCK), sink_ref[kv * SWA_GROUP + g] * LOG2E, F32)
         for g in range(SWA_GROUP)], axis=1) for kv in range(SWA_KV_HEADS)]
    units = [(blk, kv) for blk in range(SWA_Q_TILE // SWA_BLOCK) for kv in range(SWA_KV_HEADS)]
    band = lambda blk: slice(blk * SWA_BLOCK, (blk + 2) * SWA_BLOCK)
    rows = lambda blk: slice(blk * SWA_BLOCK, (blk + 1) * SWA_BLOCK)
    heads = lambda kv: slice(kv * SWA_HEAD_DIM, (kv + 1) * SWA_HEAD_DIM)
    scores, maxes = [], []
    for blk, kv in units:
        kh = k_all[band(blk), heads(kv)]
        k2 = jnp.concatenate([kh, kh], axis=1)
        q_heads = []
        for j in range(pairs_per_kv):
            hp = kv * pairs_per_kv + j
            q_pair = q_ref[rows(blk), hp * LANES:(hp + 1) * LANES]
            q_heads += [jnp.where(even_lanes, q_pair, jnp.zeros_like(q_pair)),
                        jnp.where(even_lanes, jnp.zeros_like(q_pair), q_pair)]
        s2 = _dot_nt(k2, jnp.concatenate(q_heads, axis=0))
        s = jnp.where(from_prev, s2[:SWA_BLOCK], s2[SWA_BLOCK:])
        scores.append(s)
        maxes.append(jnp.maximum(jnp.max(s, axis=0, keepdims=True), sink_rows[kv]))
    probs = []
    for u in range(len(units)):
        p = jnp.exp2(scores[u] - maxes[u])
        probs.append(jnp.concatenate([jnp.where(from_prev, p, 0.0),
                                      jnp.where(from_prev, 0.0, p)], axis=0).astype(BF16))
    outs = []
    for u, (blk, kv) in enumerate(units):
        v_ext = jnp.concatenate([v_all[band(blk), heads(kv)], one], axis=1)
        pv = lax.dot_general(v_ext, probs[u], (((0,), (0,)), ((), ())),
                             preferred_element_type=F32)
        den = pv[SWA_HEAD_DIM:] + jnp.exp2(sink_rows[kv] - maxes[u])
        outs.append(pv[:SWA_HEAD_DIM] / den)
    for u, (blk, kv) in enumerate(units):
        for j in range(pairs_per_kv):
            hp = kv * pairs_per_kv + j
            g0 = j * 2 * SWA_BLOCK
            pair = jnp.concatenate([outs[u][:, g0:g0 + SWA_BLOCK],
                                    outs[u][:, g0 + SWA_BLOCK:g0 + 2 * SWA_BLOCK]], axis=0)
            o_ref[rows(blk), hp * LANES:(hp + 1) * LANES] = pair.T.astype(BF16)


def _swa_attn(sinks, q, k, v, batch, seq):
    nt = seq // SWA_Q_TILE
    per = SWA_Q_TILE // SWA_BLOCK
    cur = lambda w: pl.BlockSpec((SWA_Q_TILE, w), lambda b, n, *_: (b * nt + n, 0))
    prev = lambda w: pl.BlockSpec(
        (SWA_BLOCK, w), lambda b, n, *_: (b * nt * per + jnp.maximum(n * per - 1, 0), 0))
    return pl.pallas_call(
        _swa_attn_kernel,
        grid_spec=pltpu.PrefetchScalarGridSpec(
            num_scalar_prefetch=1,
            grid=(batch, nt),
            in_specs=[cur(D_MODEL), prev(SWA_KV), cur(SWA_KV), prev(SWA_KV), cur(SWA_KV)],
            out_specs=cur(D_MODEL)),
        out_shape=jax.ShapeDtypeStruct((batch * seq, D_MODEL), BF16),
        compiler_params=pltpu.CompilerParams(
            dimension_semantics=("parallel", "parallel"), vmem_limit_bytes=VMEM_LIMIT),
        name="swa_attn",
    )(sinks, q, k, k, v, v)


def _post_mixer_kernel(*refs, has_out_bias, next_qkv):
    it = iter(refs)
    mix_ref, h_ref, p_ref, wo_ref = next(it), next(it), next(it), next(it)
    bo_ref = next(it) if has_out_bias else None
    g1_ref, b1_ref, wup_ref, wdn_ref, g2_ref, b2_ref, wg_ref, bg_ref, wp_ref = (
        next(it) for _ in range(9))
    wqkv_ref, bqkv_ref = (next(it), next(it)) if next_qkv else (None, None)
    out_ref = next(it)
    q_ref, k_ref, v_ref = (next(it), next(it), next(it)) if next_qkv else (None, None, None)

    half = POST_TILE // 2
    rows = [slice(0, half), slice(half, POST_TILE)]

    def out_proj(r):
        y = DEEPNORM_ALPHA * h_ref[r, :] + _dot(mix_ref[r, :], wo_ref[...])
        return y + bo_ref[...] if has_out_bias else y

    def mlp(h1b, acc, chunks):
        for c in chunks:
            u = jnp.maximum(_dot(h1b, wup_ref[:, c * FF_TILE:(c + 1) * FF_TILE]), 0.0)
            acc = acc + _dot((u * u).astype(BF16), wdn_ref[c * FF_TILE:(c + 1) * FF_TILE, :])
        return acc

    def ple(r, h2):
        z = _dot(h2.astype(BF16), wg_ref[...]) + bg_ref[...]
        gate = 1.0 / (1.0 + jnp.exp(-z))
        out = h2 + gate * _dot(p_ref[r, :].astype(BF16), wp_ref[...])
        out_ref[r, :] = out
        if next_qkv:
            ob = out.astype(BF16)
            kcol, vcol = D_MODEL, D_MODEL + SWA_KV
            q = _dot(ob, wqkv_ref[:, :kcol]) + bqkv_ref[:, :kcol]
            q_ref[r, :] = (q * (SWA_HEAD_DIM ** -0.5 * LOG2E)).astype(BF16)
            k_ref[r, :] = (_dot(ob, wqkv_ref[:, kcol:vcol]) + bqkv_ref[:, kcol:vcol]).astype(BF16)
            v_ref[r, :] = (_dot(ob, wqkv_ref[:, vcol:]) + bqkv_ref[:, vcol:]).astype(BF16)

    n_ff = D_FF // FF_TILE
    y_a = out_proj(rows[0])
    y_b = out_proj(rows[1])
    h1_a = _layer_norm(y_a, g1_ref[...], b1_ref[...])
    h1b_a = h1_a.astype(BF16)
    acc_a = mlp(h1b_a, DEEPNORM_ALPHA * h1_a, range(0, 1))
    h1_b = _layer_norm(y_b, g1_ref[...], b1_ref[...])
    h1b_b = h1_b.astype(BF16)
    acc_a = mlp(h1b_a, acc_a, range(1, n_ff))
    acc_b = mlp(h1b_b, DEEPNORM_ALPHA * h1_b, range(0, 1))
    h2_a = _layer_norm(acc_a, g2_ref[...], b2_ref[...])
    acc_b = mlp(h1b_b, acc_b, range(1, n_ff))
    h2_b = _layer_norm(acc_b, g2_ref[...], b2_ref[...])
    ple(rows[0], h2_a)
    ple(rows[1], h2_b)


def _post_mixer(mix, h2d, p3, layer, weights, out_bias=None, qkv=None):
    t = h2d.shape[0]
    row = lambda w: pl.BlockSpec((POST_TILE, w), lambda i: (i, 0))
    wo, rest = weights[0], tuple(weights[1:])
    consts = (wo,) + ((out_bias,) if out_bias is not None else ()) + rest + (qkv or ())
    out_specs = [row(D_MODEL)]
    out_shape = [jax.ShapeDtypeStruct((t, D_MODEL), F32)]
    if qkv is not None:
        out_specs += [row(D_MODEL), row(SWA_KV), row(SWA_KV)]
        out_shape += [jax.ShapeDtypeStruct((t, D_MODEL), BF16),
                      jax.ShapeDtypeStruct((t, SWA_KV), BF16),
                      jax.ShapeDtypeStruct((t, SWA_KV), BF16)]
    return pl.pallas_call(
        functools.partial(_post_mixer_kernel, has_out_bias=out_bias is not None,
                          next_qkv=qkv is not None),
        grid=(t // POST_TILE,),
        in_specs=[row(D_MODEL), row(D_MODEL),
                  pl.BlockSpec((None, POST_TILE, PLE_DIM), lambda i: (layer, i, 0))]
                 + [_resident(w.shape, layer if w.ndim == 3 else None) for w in consts],
        out_specs=out_specs,
        out_shape=out_shape,
        compiler_params=pltpu.CompilerParams(
            dimension_semantics=("parallel",), vmem_limit_bytes=VMEM_LIMIT),
        name=f"post_mixer_{layer}",
    )(mix, h2d, p3, *consts)


def kernel(x, p, gla_w_in, gla_w_gk_up, gla_b_gk, gla_norm_g, gla_w_out, swa_w_qkv, swa_b_qkv,
           swa_sinks, swa_w_out, swa_b_out, mlp_w_up, mlp_w_down, ln1_g, ln1_b, ln2_g, ln2_b,
           ple_w_proj, ple_w_gate, ple_b_gate):
    batch, seq, d = x.shape
    t = batch * seq
    x2 = x.reshape(t, d)
    p3 = p.reshape(DEPTH, t, PLE_DIM)
    vec = lambda a: a.reshape(1, -1).astype(F32)

    flat = lambda w: w.reshape(-1, w.shape[-1])
    norm_g = vec(jnp.tile(gla_norm_g[0], GLA_HEADS))
    mix, w_up, w_down, w_gate, w_proj, w_out0, w_out1, w_qkv = _gla(
        x2, gla_w_in[0].astype(BF16), gla_w_gk_up[0], vec(gla_b_gk[0]), norm_g, seq,
        cast=[flat(mlp_w_up), flat(mlp_w_down), flat(ple_w_gate), flat(ple_w_proj),
              gla_w_out[0], swa_w_out[0], swa_w_qkv[0]])
    w_up, w_down = w_up.reshape(mlp_w_up.shape), w_down.reshape(mlp_w_down.shape)
    w_gate, w_proj = w_gate.reshape(ple_w_gate.shape), w_proj.reshape(ple_w_proj.shape)

    per_layer = lambda a: a.reshape(DEPTH, 1, -1)

    def post_weights(wo):
        return (wo, per_layer(ln1_g), per_layer(ln1_b), w_up, w_down,
                per_layer(ln2_g), per_layer(ln2_b), w_gate, per_layer(ple_b_gate), w_proj)

    h, q, k, v = _post_mixer(mix, x2, p3, 0, post_weights(w_out0),
                             qkv=(w_qkv, vec(swa_b_qkv[0])))

    mix = _swa_attn(swa_sinks[0].astype(F32), q, k, v, batch, seq)
    (h,) = _post_mixer(mix, h, p3, 1, post_weights(w_out1), out_bias=vec(swa_b_out[0]))
    return h.reshape(batch, seq, d)
```

```python
import functools

import jax
import jax.numpy as jnp
from jax import lax
from jax.experimental import pallas as pl
from jax.experimental.pallas import tpu as pltpu

D_MODEL = 1024
DEPTH = 2
PLE_DIM = 256

GLA_HEADS = 4
GLA_DK = 128
GLA_DV = 256
GLA_LOWRANK = 16
GLA_TAU = 16.0
GLA_QK = GLA_HEADS * GLA_DK
GLA_V = GLA_HEADS * GLA_DV

SWA_HEAD_DIM = 64
SWA_Q_HEADS = 16
SWA_KV_HEADS = 4
SWA_GROUP = SWA_Q_HEADS // SWA_KV_HEADS
SWA_BLOCK = 128
SWA_Q_TILE = 1024
SWA_KV = SWA_KV_HEADS * SWA_HEAD_DIM

D_FF = 4 * D_MODEL
DEEPNORM_ALPHA = (2.0 * DEPTH) ** 0.25
LN_EPS = 1e-5
RMS_EPS = 1e-5
LOG2E = 1.4426950408889634

LANES = 128
BF16_SUBLANES = 16
GLA_CHUNK = 128
ROW_TILE = 512
POST_TILE = 512
FF_TILE = 1024
VMEM_LIMIT = 60 * 1024 * 1024

BF16 = jnp.bfloat16
F32 = jnp.float32


def _dot(a, b):
    return jnp.dot(a, b, preferred_element_type=F32)


def _dot_nt(a, b):
    return lax.dot_general(a, b, (((1,), (1,)), ((), ())), preferred_element_type=F32)


def _resident(shape, layer=None):
    if layer is None:
        return pl.BlockSpec(shape, lambda *_: (0, 0), pipeline_mode=pl.Buffered(1))
    return pl.BlockSpec((None,) + tuple(shape[1:]), lambda *_: (layer, 0, 0),
                        pipeline_mode=pl.Buffered(1))


def _layer_norm(y, g, b):
    mu = jnp.mean(y, axis=-1, keepdims=True)
    yc = y - mu
    var = jnp.mean(yc * yc, axis=-1, keepdims=True)
    return yc * lax.rsqrt(var + LN_EPS) * g + b


def _gla_kernel(*refs, n_cast, tiles_per_seq):
    x_ref, w_ref, wup_ref, bgk_ref, ng_ref = refs[:5]
    cast_in = refs[5:5 + n_cast]
    o_ref = refs[5 + n_cast]
    cast_out = refs[6 + n_cast:6 + 2 * n_cast]
    q_s, k_s, v_s, g_s, b_s, state_ref = refs[6 + 2 * n_cast:]
    i = pl.program_id(0)

    @pl.when(i == 0)
    def _():
        q_s[1] = jnp.zeros(q_s.shape[1:], q_s.dtype)
        k_s[1] = jnp.zeros(k_s.shape[1:], k_s.dtype)
        v_s[1] = jnp.zeros(v_s.shape[1:], v_s.dtype)
        g_s[1] = jnp.zeros(g_s.shape[1:], g_s.dtype)
        b_s[1] = jnp.zeros(b_s.shape[1:], b_s.dtype)
        state_ref[...] = jnp.zeros_like(state_ref)

    for parity in range(2):
        @pl.when(i % 2 == parity)
        def _():
            _gla_step(x_ref, w_ref, wup_ref, bgk_ref, ng_ref, o_ref, q_s, k_s, v_s, g_s, b_s,
                      state_ref, cast_in, cast_out, wr=parity, tiles_per_seq=tiles_per_seq)


def _gla_step(x_ref, w_ref, wup_ref, bgk_ref, ng_ref, o_ref,
              q_s, k_s, v_s, g_s, b_s, state_ref, cast_in, cast_out, *, wr, tiles_per_seq):
    i = pl.program_id(0)
    rd = 1 - wr
    c_k, c_v, c_r, c_gl = GLA_QK, 2 * GLA_QK, 2 * GLA_QK + GLA_V, 2 * GLA_QK + 2 * GLA_V
    row = lax.broadcasted_iota(jnp.int32, (GLA_CHUNK, GLA_CHUNK), 0)
    col = lax.broadcasted_iota(jnp.int32, (GLA_CHUNK, GLA_CHUNK), 1)
    causal = col <= row
    tril2 = jnp.concatenate([causal.astype(BF16)] * 2, axis=1)
    heads = range(GLA_HEADS)
    ks = [slice(h * GLA_DK, (h + 1) * GLA_DK) for h in heads]
    vs = [slice(h * GLA_DV, (h + 1) * GLA_DV) for h in heads]

    fresh = (i - 1) % tiles_per_seq == 0
    state = [jnp.where(fresh, 0.0, state_ref[h]) for h in heads]

    n_chunks = ROW_TILE // GLA_CHUNK
    chunk_rows = [slice(c * GLA_CHUNK, (c + 1) * GLA_CHUNK) for c in range(n_chunks)]

    def prepare(c):
        out = []
        for h in heads:
            b = b_s[rd, chunk_rows[c], ks[h]]
            b_last = b[GLA_CHUNK - 1:GLA_CHUNK, :]
            half = 0.5 * b_last
            e_half = jnp.exp(half).astype(BF16)
            bq = b - half
            qd = q_s[rd, chunk_rows[c], ks[h]] * jnp.exp(bq).astype(BF16)
            kd = k_s[rd, chunk_rows[c], ks[h]] * jnp.exp(-bq).astype(BF16)
            dec = jnp.broadcast_to(jnp.exp(b_last), (GLA_CHUNK, GLA_DK)).T
            out.append(dict(att=_dot_nt(qd, kd), q_in=qd * e_half,
                            kl=kd * e_half, dec=dec))
        return out

    half_v = GLA_V // 2
    xb = x_ref[...].astype(BF16)
    gl = _dot(xb, w_ref[:, c_gl:]).astype(BF16)
    prep = [prepare(0)]
    r0 = _dot(xb, w_ref[:, c_r:c_r + half_v])
    prep.append(prepare(1))
    r1 = _dot(xb, w_ref[:, c_r + half_v:c_gl])
    z = _dot(gl, wup_ref[...].astype(BF16)) + bgk_ref[...]
    prep.append(prepare(2))
    q_s[wr] = (_dot(xb, w_ref[:, :c_k]) * (GLA_DK ** -0.5)).astype(BF16)
    prep.append(prepare(3))
    k_s[wr] = _dot(xb, w_ref[:, c_k:c_v]).astype(BF16)
    for src, dst in zip(cast_in, cast_out):
        dst[...] = src[...].astype(BF16)

    upd = [[lax.dot_general(prep[c][h]["kl"], v_s[rd, chunk_rows[c], vs[h]],
                            (((0,), (0,)), ((), ())), preferred_element_type=F32)
            for h in heads] for c in range(n_chunks)]
    r = jnp.concatenate([r0, r1], axis=1)
    g_s[wr] = (r / (1.0 + jnp.exp(-r)) * ng_ref[...]).astype(BF16)
    states = [state]
    for c in range(n_chunks):
        states.append([jnp.concatenate(
            [states[c][h][:, j * GLA_CHUNK:(j + 1) * GLA_CHUNK] * prep[c][h]["dec"]
             for j in range(GLA_DV // GLA_CHUNK)], axis=1) + upd[c][h] for h in heads])
    v_s[wr, :, :half_v] = _dot(xb, w_ref[:, c_v:c_v + half_v]).astype(BF16)

    o = []
    for c in range(n_chunks):
        o.append([_dot(
            jnp.concatenate([jnp.where(causal, prep[c][h]["att"], 0.0).astype(BF16),
                             prep[c][h]["q_in"]], axis=1),
            jnp.concatenate([v_s[rd, chunk_rows[c], vs[h]], states[c][h].astype(BF16)], axis=0))
            for h in heads])
        if c == 1:
            v_s[wr, :, half_v:] = _dot(xb, w_ref[:, c_v + half_v:c_r]).astype(BF16)
    log_a = (jnp.minimum(z, 0.0) - jnp.log1p(jnp.exp(-jnp.abs(z)))) * (1.0 / GLA_TAU)
    hi = log_a.astype(BF16)
    lo = (log_a - hi.astype(F32)).astype(BF16)
    for c in range(n_chunks):
        b_s[wr, chunk_rows[c], :] = _dot(
            tril2, jnp.concatenate([hi[chunk_rows[c]], lo[chunk_rows[c]]], axis=0))
    for c in range(n_chunks):
        for h in heads:
            inv_rms = lax.rsqrt(jnp.mean(o[c][h] * o[c][h], axis=-1, keepdims=True) + RMS_EPS)
            gate = g_s[rd, chunk_rows[c], vs[h]].astype(F32)
            o_ref[chunk_rows[c], vs[h]] = (o[c][h] * inv_rms * gate).astype(BF16)
    state = states[n_chunks]
    for h in heads:
        state_ref[h] = state[h]


def _gla(x2, w_in, wup, bgk, ng, seq, cast):
    t = x2.shape[0]
    n = t // ROW_TILE
    assert ROW_TILE // GLA_CHUNK == 4 and seq % ROW_TILE == 0

    def cast_spec(w):
        rows = max(BF16_SUBLANES, w.shape[0] // n)
        last = w.shape[0] // rows - 1
        return pl.BlockSpec((rows, w.shape[1]), lambda i: (jnp.minimum(i, last), 0))

    cast_specs = [cast_spec(w) for w in cast]
    return pl.pallas_call(
        functools.partial(_gla_kernel, n_cast=len(cast), tiles_per_seq=seq // ROW_TILE),
        grid=(n + 1,),
        in_specs=[pl.BlockSpec((ROW_TILE, D_MODEL), lambda i: (jnp.minimum(i, n - 1), 0)),
                  _resident(w_in.shape), _resident(wup.shape), _resident(bgk.shape),
                  _resident(ng.shape)] + cast_specs,
        out_specs=[pl.BlockSpec((ROW_TILE, GLA_V), lambda i: (jnp.maximum(i - 1, 0), 0))]
                  + cast_specs,
        out_shape=[jax.ShapeDtypeStruct((t, GLA_V), BF16)]
                  + [jax.ShapeDtypeStruct(w.shape, BF16) for w in cast],
        scratch_shapes=[pltpu.VMEM((2, ROW_TILE, GLA_QK), BF16),
                        pltpu.VMEM((2, ROW_TILE, GLA_QK), BF16),
                        pltpu.VMEM((2, ROW_TILE, GLA_V), BF16),
                        pltpu.VMEM((2, ROW_TILE, GLA_V), BF16),
                        pltpu.VMEM((2, ROW_TILE, GLA_QK), F32),
                        pltpu.VMEM((GLA_HEADS, GLA_DK, GLA_DV), F32)],
        compiler_params=pltpu.CompilerParams(
            dimension_semantics=("arbitrary",), vmem_limit_bytes=VMEM_LIMIT),
        name="gla",
    )(x2, w_in, wup, bgk, ng, *cast)


def _swa_attn_kernel(sink_ref, q_ref, kp_ref, kc_ref, vp_ref, vc_ref, o_ref):
    has_prev = (pl.program_id(1) > 0).astype(BF16)
    k_all = jnp.concatenate([kp_ref[...] * has_prev, kc_ref[...]], axis=0)
    v_all = jnp.concatenate([vp_ref[...] * has_prev, vc_ref[...]], axis=0)
    grp = SWA_GROUP * SWA_BLOCK
    kj = lax.broadcasted_iota(jnp.int32, (SWA_BLOCK, grp), 0)
    qi = lax.broadcasted_iota(jnp.int32, (SWA_BLOCK, grp), 1) % SWA_BLOCK
    from_prev = kj > qi
    lane = lax.broadcasted_iota(jnp.int32, (SWA_BLOCK, LANES), 1)
    even_lanes = lane < SWA_HEAD_DIM
    one = jnp.ones((2 * SWA_BLOCK, SWA_HEAD_DIM), BF16)
    pairs_per_kv = SWA_GROUP // 2
    sink_rows = [jnp.concatenate(
        [jnp.full((1, SWA_BLOCK), sink_ref[kv * SWA_GROUP + g] * LOG2E, F32)
         for g in range(SWA_GROUP)], axis=1) for kv in range(SWA_KV_HEADS)]
    units = [(blk, kv) for blk in range(SWA_Q_TILE // SWA_BLOCK) for kv in range(SWA_KV_HEADS)]
    band = lambda blk: slice(blk * SWA_BLOCK, (blk + 2) * SWA_BLOCK)
    rows = lambda blk: slice(blk * SWA_BLOCK, (blk + 1) * SWA_BLOCK)
    heads = lambda kv: slice(kv * SWA_HEAD_DIM, (kv + 1) * SWA_HEAD_DIM)
    scores, maxes = [], []
    for blk, kv in units:
        kh = k_all[band(blk), heads(kv)]
        k2 = jnp.concatenate([kh, kh], axis=1)
        q_heads = []
        for j in range(pairs_per_kv):
            hp = kv * pairs_per_kv + j
            q_pair = q_ref[rows(blk), hp * LANES:(hp + 1) * LANES]
            q_heads += [jnp.where(even_lanes, q_pair, jnp.zeros_like(q_pair)),
                        jnp.where(even_lanes, jnp.zeros_like(q_pair), q_pair)]
        s2 = _dot_nt(k2, jnp.concatenate(q_heads, axis=0))
        s = jnp.where(from_prev, s2[:SWA_BLOCK], s2[SWA_BLOCK:])
        scores.append(s)
        maxes.append(jnp.maximum(jnp.max(s, axis=0, keepdims=True), sink_rows[kv]))
    probs = []
    for u in range(len(units)):
        p = jnp.exp2(scores[u] - maxes[u])
        probs.append(jnp.concatenate([jnp.where(from_prev, p, 0.0),
                                      jnp.where(from_prev, 0.0, p)], axis=0).astype(BF16))
    outs = []
    for u, (blk, kv) in enumerate(units):
        v_ext = jnp.concatenate([v_all[band(blk), heads(kv)], one], axis=1)
        pv = lax.dot_general(v_ext, probs[u], (((0,), (0,)), ((), ())),
                             preferred_element_type=F32)
        den = pv[SWA_HEAD_DIM:] + jnp.exp2(sink_rows[kv] - maxes[u])
        outs.append(pv[:SWA_HEAD_DIM] / den)
    for u, (blk, kv) in enumerate(units):
        for j in range(pairs_per_kv):
            hp = kv * pairs_per_kv + j
            g0 = j * 2 * SWA_BLOCK
            pair = jnp.concatenate([outs[u][:, g0:g0 + SWA_BLOCK],
                                    outs[u][:, g0 + SWA_BLOCK:g0 + 2 * SWA_BLOCK]], axis=0)
            o_ref[rows(blk), hp * LANES:(hp + 1) * LANES] = pair.T.astype(BF16)


def _swa_attn(sinks, q, k, v, batch, seq):
    nt = seq // SWA_Q_TILE
    per = SWA_Q_TILE // SWA_BLOCK
    cur = lambda w: pl.BlockSpec((SWA_Q_TILE, w), lambda b, n, *_: (b * nt + n, 0))
    prev = lambda w: pl.BlockSpec(
        (SWA_BLOCK, w), lambda b, n, *_: (b * nt * per + jnp.maximum(n * per - 1, 0), 0))
    return pl.pallas_call(
        _swa_attn_kernel,
        grid_spec=pltpu.PrefetchScalarGridSpec(
            num_scalar_prefetch=1,
            grid=(batch, nt),
            in_specs=[cur(D_MODEL), prev(SWA_KV), cur(SWA_KV), prev(SWA_KV), cur(SWA_KV)],
            out_specs=cur(D_MODEL)),
        out_shape=jax.ShapeDtypeStruct((batch * seq, D_MODEL), BF16),
        compiler_params=pltpu.CompilerParams(
            dimension_semantics=("parallel", "parallel"), vmem_limit_bytes=VMEM_LIMIT),
        name="swa_attn",
    )(sinks, q, k, k, v, v)


def _post_mixer_kernel(*refs, layer, has_out_bias, next_qkv):
    it = iter(refs)
    mix_ref, h_ref, p_ref, wo_ref = next(it), next(it), next(it), next(it)
    bo_ref = next(it) if has_out_bias else None
    g1_all, b1_all, wup_ref, wdn_ref, g2_all, b2_all, wg_ref, bg_all, wp_ref = (
        next(it) for _ in range(9))
    g1_ref, b1_ref, g2_ref, b2_ref, bg_ref = (
        ref.at[layer:layer + 1, :] for ref in (g1_all, b1_all, g2_all, b2_all, bg_all))
    wqkv_ref, bqkv_ref = (next(it), next(it)) if next_qkv else (None, None)
    out_ref = next(it)
    q_ref, k_ref, v_ref = (next(it), next(it), next(it)) if next_qkv else (None, None, None)

    half = POST_TILE // 2
    rows = [slice(0, half), slice(half, POST_TILE)]

    def out_proj(r):
        y = DEEPNORM_ALPHA * h_ref[r, :] + _dot(mix_ref[r, :], wo_ref[...])
        return y + bo_ref[...] if has_out_bias else y

    def mlp(h1b, acc, chunks):
        for c in chunks:
            u = jnp.maximum(_dot(h1b, wup_ref[:, c * FF_TILE:(c + 1) * FF_TILE]), 0.0)
            acc = acc + _dot((u * u).astype(BF16), wdn_ref[c * FF_TILE:(c + 1) * FF_TILE, :])
        return acc

    def ple(r, h2):
        z = _dot(h2.astype(BF16), wg_ref[...]) + bg_ref[...]
        gate = 1.0 / (1.0 + jnp.exp(-z))
        out = h2 + gate * _dot(p_ref[r, :].astype(BF16), wp_ref[...])
        out_ref[r, :] = out
        if next_qkv:
            ob = out.astype(BF16)
            kcol, vcol = D_MODEL, D_MODEL + SWA_KV
            q = _dot(ob, wqkv_ref[:, :kcol]) + bqkv_ref[:, :kcol]
            q_ref[r, :] = (q * (SWA_HEAD_DIM ** -0.5 * LOG2E)).astype(BF16)
            k_ref[r, :] = (_dot(ob, wqkv_ref[:, kcol:vcol]) + bqkv_ref[:, kcol:vcol]).astype(BF16)
            v_ref[r, :] = (_dot(ob, wqkv_ref[:, vcol:]) + bqkv_ref[:, vcol:]).astype(BF16)

    n_ff = D_FF // FF_TILE
    y_a = out_proj(rows[0])
    y_b = out_proj(rows[1])
    h1_a = _layer_norm(y_a, g1_ref[...], b1_ref[...])
    h1b_a = h1_a.astype(BF16)
    acc_a = mlp(h1b_a, DEEPNORM_ALPHA * h1_a, range(0, 1))
    h1_b = _layer_norm(y_b, g1_ref[...], b1_ref[...])
    h1b_b = h1_b.astype(BF16)
    acc_a = mlp(h1b_a, acc_a, range(1, n_ff))
    acc_b = mlp(h1b_b, DEEPNORM_ALPHA * h1_b, range(0, 1))
    h2_a = _layer_norm(acc_a, g2_ref[...], b2_ref[...])
    acc_b = mlp(h1b_b, acc_b, range(1, n_ff))
    h2_b = _layer_norm(acc_b, g2_ref[...], b2_ref[...])
    ple(rows[0], h2_a)
    ple(rows[1], h2_b)


def _post_mixer(mix, h2d, p3, layer, weights, out_bias=None, qkv=None):
    t = h2d.shape[0]
    row = lambda w: pl.BlockSpec((POST_TILE, w), lambda i: (i, 0))
    wo, rest = weights[0], tuple(weights[1:])
    consts = (wo,) + ((out_bias,) if out_bias is not None else ()) + rest + (qkv or ())
    out_specs = [row(D_MODEL)]
    out_shape = [jax.ShapeDtypeStruct((t, D_MODEL), F32)]
    if qkv is not None:
        out_specs += [row(D_MODEL), row(SWA_KV), row(SWA_KV)]
        out_shape += [jax.ShapeDtypeStruct((t, D_MODEL), BF16),
                      jax.ShapeDtypeStruct((t, SWA_KV), BF16),
                      jax.ShapeDtypeStruct((t, SWA_KV), BF16)]
    return pl.pallas_call(
        functools.partial(_post_mixer_kernel, layer=layer, has_out_bias=out_bias is not None,
                          next_qkv=qkv is not None),
        grid=(t // POST_TILE,),
        in_specs=[row(D_MODEL), row(D_MODEL),
                  pl.BlockSpec((None, POST_TILE, PLE_DIM), lambda i: (layer, i, 0))]
                 + [_resident(w.shape, layer if w.ndim == 3 else None) for w in consts],
        out_specs=out_specs,
        out_shape=out_shape,
        compiler_params=pltpu.CompilerParams(
            dimension_semantics=("parallel",), vmem_limit_bytes=VMEM_LIMIT),
        name=f"post_mixer_{layer}",
    )(mix, h2d, p3, *consts)


def kernel(x, p, gla_w_in, gla_w_gk_up, gla_b_gk, gla_norm_g, gla_w_out, swa_w_qkv, swa_b_qkv,
           swa_sinks, swa_w_out, swa_b_out, mlp_w_up, mlp_w_down, ln1_g, ln1_b, ln2_g, ln2_b,
           ple_w_proj, ple_w_gate, ple_b_gate):
    batch, seq, d = x.shape
    t = batch * seq
    x2 = x.reshape(t, d)
    p3 = p.reshape(DEPTH, t, PLE_DIM)
    vec = lambda a: a.reshape(1, -1).astype(F32)

    flat = lambda w: w.reshape(-1, w.shape[-1])
    norm_g = vec(jnp.tile(gla_norm_g[0], GLA_HEADS))
    mix, w_up, w_down, w_gate, w_proj, w_out0, w_out1, w_qkv = _gla(
        x2, gla_w_in[0].astype(BF16), gla_w_gk_up[0], vec(gla_b_gk[0]), norm_g, seq,
        cast=[flat(mlp_w_up), flat(mlp_w_down), flat(ple_w_gate), flat(ple_w_proj),
              gla_w_out[0], swa_w_out[0], swa_w_qkv[0]])
    w_up, w_down = w_up.reshape(mlp_w_up.shape), w_down.reshape(mlp_w_down.shape)
    w_gate, w_proj = w_gate.reshape(ple_w_gate.shape), w_proj.reshape(ple_w_proj.shape)

    def post_weights(wo):
        return (wo, ln1_g, ln1_b, w_up, w_down, ln2_g, ln2_b, w_gate, ple_b_gate, w_proj)

    h, q, k, v = _post_mixer(mix, x2, p3, 0, post_weights(w_out0),
                             qkv=(w_qkv, vec(swa_b_qkv[0])))

    mix = _swa_attn(swa_sinks[0].astype(F32), q, k, v, batch, seq)
    (h,) = _post_mixer(mix, h, p3, 1, post_weights(w_out1), out_bias=vec(swa_b_out[0]))
    return h.reshape(batch, seq, d)
```

```python
import functools

import jax
import jax.numpy as jnp
from jax import lax
from jax.experimental import pallas as pl
from jax.experimental.pallas import tpu as pltpu

D_MODEL = 1024
DEPTH = 2
PLE_DIM = 256

GLA_HEADS = 4
GLA_DK = 128
GLA_DV = 256
GLA_LOWRANK = 16
GLA_TAU = 16.0
GLA_QK = GLA_HEADS * GLA_DK
GLA_V = GLA_HEADS * GLA_DV

SWA_HEAD_DIM = 64
SWA_Q_HEADS = 16
SWA_KV_HEADS = 4
SWA_GROUP = SWA_Q_HEADS // SWA_KV_HEADS
SWA_BLOCK = 128
SWA_Q_TILE = 1024
SWA_KV = SWA_KV_HEADS * SWA_HEAD_DIM

D_FF = 4 * D_MODEL
DEEPNORM_ALPHA = (2.0 * DEPTH) ** 0.25
LN_EPS = 1e-5
RMS_EPS = 1e-5
LOG2E = 1.4426950408889634

LANES = 128
BF16_SUBLANES = 16
GLA_CHUNK = 128
GLA_LEVELS = 7
GLA_COARSE_LEVELS = 4
ROW_TILE = 512
POST_TILE = 512
FF_TILE = 1024
VMEM_LIMIT = 60 * 1024 * 1024

BF16 = jnp.bfloat16
F32 = jnp.float32


def _dot(a, b):
    return jnp.dot(a, b, preferred_element_type=F32)


def _dot_nt(a, b):
    return lax.dot_general(a, b, (((1,), (1,)), ((), ())), preferred_element_type=F32)


def _resident(shape, layer=None):
    if layer is None:
        return pl.BlockSpec(shape, lambda *_: (0, 0), pipeline_mode=pl.Buffered(1))
    return pl.BlockSpec((None,) + tuple(shape[1:]), lambda *_: (layer, 0, 0),
                        pipeline_mode=pl.Buffered(1))


def _layer_norm(y, g, b):
    mu = jnp.mean(y, axis=-1, keepdims=True)
    yc = y - mu
    var = jnp.mean(yc * yc, axis=-1, keepdims=True)
    return yc * lax.rsqrt(var + LN_EPS) * g + b


def _gla_kernel(*refs, n_cast, tiles_per_seq):
    x_ref, w_ref, wup_ref, bgk_ref, ng_ref = refs[:5]
    cast_in = refs[5:5 + n_cast]
    o_ref = refs[5 + n_cast]
    cast_out = refs[6 + n_cast:6 + 2 * n_cast]
    q_s, k_s, v_s, g_s, b_s, e_s, state_ref = refs[6 + 2 * n_cast:]
    i = pl.program_id(0)

    @pl.when(i == 0)
    def _():
        q_s[1] = jnp.zeros(q_s.shape[1:], q_s.dtype)
        k_s[1] = jnp.zeros(k_s.shape[1:], k_s.dtype)
        v_s[1] = jnp.zeros(v_s.shape[1:], v_s.dtype)
        g_s[1] = jnp.zeros(g_s.shape[1:], g_s.dtype)
        b_s[1] = jnp.zeros(b_s.shape[1:], b_s.dtype)
        for l in range(GLA_LEVELS - GLA_COARSE_LEVELS):
            e_s[l, 1] = jnp.zeros(e_s.shape[2:], e_s.dtype)
        state_ref[...] = jnp.zeros_like(state_ref)

    for parity in range(2):
        @pl.when(i % 2 == parity)
        def _():
            _gla_step(x_ref, w_ref, wup_ref, bgk_ref, ng_ref, o_ref, q_s, k_s, v_s, g_s, b_s,
                      e_s, state_ref, cast_in, cast_out, wr=parity, tiles_per_seq=tiles_per_seq)


def _gla_step(x_ref, w_ref, wup_ref, bgk_ref, ng_ref, o_ref,
              q_s, k_s, v_s, g_s, b_s, e_s, state_ref, cast_in, cast_out, *, wr, tiles_per_seq):
    i = pl.program_id(0)
    rd = 1 - wr
    c_k, c_v, c_r, c_gl = GLA_QK, 2 * GLA_QK, 2 * GLA_QK + GLA_V, 2 * GLA_QK + 2 * GLA_V
    row = lax.broadcasted_iota(jnp.int32, (GLA_CHUNK, GLA_CHUNK), 0)
    col = lax.broadcasted_iota(jnp.int32, (GLA_CHUNK, GLA_CHUNK), 1)
    causal = col <= row
    tril2 = jnp.concatenate([causal.astype(BF16)] * 2, axis=1)
    heads = range(GLA_HEADS)
    ks = [slice(h * GLA_DK, (h + 1) * GLA_DK) for h in heads]
    vs = [slice(h * GLA_DV, (h + 1) * GLA_DV) for h in heads]

    fresh = (i - 1) % tiles_per_seq == 0
    state = [jnp.where(fresh, 0.0, state_ref[h]) for h in heads]

    n_chunks = ROW_TILE // GLA_CHUNK
    chunk_rows = [slice(c * GLA_CHUNK, (c + 1) * GLA_CHUNK) for c in range(n_chunks)]

    levels = [GLA_CHUNK >> l for l in range(GLA_LEVELS)]
    upper = [(row % blk) >= blk // 2 for blk in levels]
    pair_ok = [((row - row % blk) == (col - col % blk)) & up & ((col % blk) < blk // 2)
               for blk, up in zip(levels, upper)]
    diagonal = row == col

    def prepare(c):
        out = []
        for h in heads:
            b = b_s[rd, chunk_rows[c], ks[h]]
            b_last = b[GLA_CHUNK - 1:GLA_CHUNK, :]
            q = q_s[rd, chunk_rows[c], ks[h]]
            k = k_s[rd, chunk_rows[c], ks[h]]
            qk = jnp.sum(q.astype(F32) * k.astype(F32), axis=-1, keepdims=True)
            att = jnp.where(diagonal, qk, 0.0)
            zero = jnp.zeros_like(q)
            for l in range(GLA_LEVELS):
                if l < GLA_COARSE_LEVELS:
                    blk = levels[l]
                    rho = jnp.concatenate(
                        [jnp.broadcast_to(b[m + blk // 2 - 1:m + blk // 2, :], (blk, GLA_DK))
                         for m in range(0, GLA_CHUNK, blk)], axis=0)
                    level_exp = -jnp.abs(b - rho)
                else:
                    level_exp = e_s[l - GLA_COARSE_LEVELS, rd, chunk_rows[c], ks[h]]
                e = jnp.exp(level_exp).astype(BF16)
                ql = jnp.where(upper[l], q * e, zero)
                kl_l = jnp.where(upper[l], zero, k * e)
                scores = _dot_nt(ql, kl_l)
                att = att + (scores if l == 0 else jnp.where(pair_ok[l], scores, 0.0))
            dec = jnp.broadcast_to(jnp.exp(b_last), (GLA_CHUNK, GLA_DK)).T
            out.append(dict(att=att.astype(BF16),
                            q_in=q * jnp.exp(b).astype(BF16),
                            kl=k * jnp.exp(b_last - b).astype(BF16),
                            dec=dec))
        return out

    half_v = GLA_V // 2
    xb = x_ref[...].astype(BF16)
    gl = _dot(xb, w_ref[:, c_gl:]).astype(BF16)
    prep = [prepare(0)]
    r0 = _dot(xb, w_ref[:, c_r:c_r + half_v])
    prep.append(prepare(1))
    r1 = _dot(xb, w_ref[:, c_r + half_v:c_gl])
    z = _dot(gl, wup_ref[...].astype(BF16)) + bgk_ref[...]
    prep.append(prepare(2))
    q_s[wr] = (_dot(xb, w_ref[:, :c_k]) * (GLA_DK ** -0.5)).astype(BF16)
    prep.append(prepare(3))
    k_s[wr] = _dot(xb, w_ref[:, c_k:c_v]).astype(BF16)
    for src, dst in zip(cast_in, cast_out):
        dst[...] = src[...].astype(BF16)

    upd = [[lax.dot_general(prep[c][h]["kl"], v_s[rd, chunk_rows[c], vs[h]],
                            (((0,), (0,)), ((), ())), preferred_element_type=F32)
            for h in heads] for c in range(n_chunks)]
    r = jnp.concatenate([r0, r1], axis=1)
    g_s[wr] = (r / (1.0 + jnp.exp(-r)) * ng_ref[...]).astype(BF16)
    states = [state]
    for c in range(n_chunks):
        states.append([jnp.concatenate(
            [states[c][h][:, j * GLA_CHUNK:(j + 1) * GLA_CHUNK] * prep[c][h]["dec"]
             for j in range(GLA_DV // GLA_CHUNK)], axis=1) + upd[c][h] for h in heads])
    v_s[wr, :, :half_v] = _dot(xb, w_ref[:, c_v:c_v + half_v]).astype(BF16)

    o = []
    for c in range(n_chunks):
        o.append([_dot(
            jnp.concatenate([prep[c][h]["att"], prep[c][h]["q_in"]], axis=1),
            jnp.concatenate([v_s[rd, chunk_rows[c], vs[h]], states[c][h].astype(BF16)], axis=0))
            for h in heads])
        if c == 1:
            v_s[wr, :, half_v:] = _dot(xb, w_ref[:, c_v + half_v:c_r]).astype(BF16)
    log_a = (jnp.minimum(z, 0.0) - jnp.log1p(jnp.exp(-jnp.abs(z)))) * (1.0 / GLA_TAU)
    hi = log_a.astype(BF16)
    lo = (log_a - hi.astype(F32)).astype(BF16)
    span = []
    for blk, up in list(zip(levels, upper))[GLA_COARSE_LEVELS:]:
        mid = row - row % blk + blk // 2
        sel = (up & (col >= mid) & (col <= row)) | (~up & (col > row) & (col < mid))
        span.append(jnp.concatenate([sel.astype(BF16)] * 2, axis=1))
    for c in range(n_chunks):
        addends = jnp.concatenate([hi[chunk_rows[c]], lo[chunk_rows[c]]], axis=0)
        b_s[wr, chunk_rows[c], :] = _dot(tril2, addends)
        for l in range(GLA_LEVELS - GLA_COARSE_LEVELS):
            e_s[l, wr, chunk_rows[c], :] = _dot(span[l], addends)
    for c in range(n_chunks):
        for h in heads:
            inv_rms = lax.rsqrt(jnp.mean(o[c][h] * o[c][h], axis=-1, keepdims=True) + RMS_EPS)
            gate = g_s[rd, chunk_rows[c], vs[h]].astype(F32)
            o_ref[chunk_rows[c], vs[h]] = (o[c][h] * inv_rms * gate).astype(BF16)
    state = states[n_chunks]
    for h in heads:
        state_ref[h] = state[h]


def _gla(x2, w_in, wup, bgk, ng, seq, cast):
    t = x2.shape[0]
    n = t // ROW_TILE
    assert ROW_TILE // GLA_CHUNK == 4 and seq % ROW_TILE == 0

    def cast_spec(w):
        rows = max(BF16_SUBLANES, w.shape[0] // n)
        last = w.shape[0] // rows - 1
        return pl.BlockSpec((rows, w.shape[1]), lambda i: (jnp.minimum(i, last), 0))

    cast_specs = [cast_spec(w) for w in cast]
    return pl.pallas_call(
        functools.partial(_gla_kernel, n_cast=len(cast), tiles_per_seq=seq // ROW_TILE),
        grid=(n + 1,),
        in_specs=[pl.BlockSpec((ROW_TILE, D_MODEL), lambda i: (jnp.minimum(i, n - 1), 0)),
                  _resident(w_in.shape), _resident(wup.shape), _resident(bgk.shape),
                  _resident(ng.shape)] + cast_specs,
        out_specs=[pl.BlockSpec((ROW_TILE, GLA_V), lambda i: (jnp.maximum(i - 1, 0), 0))]
                  + cast_specs,
        out_shape=[jax.ShapeDtypeStruct((t, GLA_V), BF16)]
                  + [jax.ShapeDtypeStruct(w.shape, BF16) for w in cast],
        scratch_shapes=[pltpu.VMEM((2, ROW_TILE, GLA_QK), BF16),
                        pltpu.VMEM((2, ROW_TILE, GLA_QK), BF16),
                        pltpu.VMEM((2, ROW_TILE, GLA_V), BF16),
                        pltpu.VMEM((2, ROW_TILE, GLA_V), BF16),
                        pltpu.VMEM((2, ROW_TILE, GLA_QK), F32),
                        pltpu.VMEM((GLA_LEVELS - GLA_COARSE_LEVELS, 2, ROW_TILE, GLA_QK), F32),
                        pltpu.VMEM((GLA_HEADS, GLA_DK, GLA_DV), F32)],
        compiler_params=pltpu.CompilerParams(
            dimension_semantics=("arbitrary",), vmem_limit_bytes=VMEM_LIMIT),
        name="gla",
    )(x2, w_in, wup, bgk, ng, *cast)


def _swa_attn_kernel(sink_ref, q_ref, kp_ref, kc_ref, vp_ref, vc_ref, o_ref):
    has_prev = (pl.program_id(1) > 0).astype(BF16)
    k_all = jnp.concatenate([kp_ref[...] * has_prev, kc_ref[...]], axis=0)
    v_all = jnp.concatenate([vp_ref[...] * has_prev, vc_ref[...]], axis=0)
    grp = SWA_GROUP * SWA_BLOCK
    kj = lax.broadcasted_iota(jnp.int32, (SWA_BLOCK, grp), 0)
    qi = lax.broadcasted_iota(jnp.int32, (SWA_BLOCK, grp), 1) % SWA_BLOCK
    from_prev = kj > qi
    lane = lax.broadcasted_iota(jnp.int32, (SWA_BLOCK, LANES), 1)
    even_lanes = lane < SWA_HEAD_DIM
    one = jnp.ones((2 * SWA_BLOCK, SWA_HEAD_DIM), BF16)
    pairs_per_kv = SWA_GROUP // 2
    sink_rows = [jnp.concatenate(
        [jnp.full((1, SWA_BLOCK), sink_ref[kv * SWA_GROUP + g] * LOG2E, F32)
         for g in range(SWA_GROUP)], axis=1) for kv in range(SWA_KV_HEADS)]
    units = [(blk, kv) for blk in range(SWA_Q_TILE // SWA_BLOCK) for kv in range(SWA_KV_HEADS)]
    band = lambda blk: slice(blk * SWA_BLOCK, (blk + 2) * SWA_BLOCK)
    rows = lambda blk: slice(blk * SWA_BLOCK, (blk + 1) * SWA_BLOCK)
    heads = lambda kv: slice(kv * SWA_HEAD_DIM, (kv + 1) * SWA_HEAD_DIM)
    scores, maxes = [], []
    for blk, kv in units:
        kh = k_all[band(blk), heads(kv)]
        k2 = jnp.concatenate([kh, kh], axis=1)
        q_heads = []
        for j in range(pairs_per_kv):
            hp = kv * pairs_per_kv + j
            q_pair = q_ref[rows(blk), hp * LANES:(hp + 1) * LANES]
            q_heads += [jnp.where(even_lanes, q_pair, jnp.zeros_like(q_pair)),
                        jnp.where(even_lanes, jnp.zeros_like(q_pair), q_pair)]
        s2 = _dot_nt(k2, jnp.concatenate(q_heads, axis=0))
        s = jnp.where(from_prev, s2[:SWA_BLOCK], s2[SWA_BLOCK:])
        scores.append(s)
        maxes.append(jnp.maximum(jnp.max(s, axis=0, keepdims=True), sink_rows[kv]))
    probs = []
    for u in range(len(units)):
        p = jnp.exp2(scores[u] - maxes[u])
        probs.append(jnp.concatenate([jnp.where(from_prev, p, 0.0),
                                      jnp.where(from_prev, 0.0, p)], axis=0).astype(BF16))
    outs = []
    for u, (blk, kv) in enumerate(units):
        v_ext = jnp.concatenate([v_all[band(blk), heads(kv)], one], axis=1)
        pv = lax.dot_general(v_ext, probs[u], (((0,), (0,)), ((), ())),
                             preferred_element_type=F32)
        den = pv[SWA_HEAD_DIM:] + jnp.exp2(sink_rows[kv] - maxes[u])
        outs.append(pv[:SWA_HEAD_DIM] / den)
    for u, (blk, kv) in enumerate(units):
        for j in range(pairs_per_kv):
            hp = kv * pairs_per_kv + j
            g0 = j * 2 * SWA_BLOCK
            pair = jnp.concatenate([outs[u][:, g0:g0 + SWA_BLOCK],
                                    outs[u][:, g0 + SWA_BLOCK:g0 + 2 * SWA_BLOCK]], axis=0)
            o_ref[rows(blk), hp * LANES:(hp + 1) * LANES] = pair.T.astype(BF16)


def _swa_attn(sinks, q, k, v, batch, seq):
    nt = seq // SWA_Q_TILE
    per = SWA_Q_TILE // SWA_BLOCK
    cur = lambda w: pl.BlockSpec((SWA_Q_TILE, w), lambda b, n, *_: (b * nt + n, 0))
    prev = lambda w: pl.BlockSpec(
        (SWA_BLOCK, w), lambda b, n, *_: (b * nt * per + jnp.maximum(n * per - 1, 0), 0))
    return pl.pallas_call(
        _swa_attn_kernel,
        grid_spec=pltpu.PrefetchScalarGridSpec(
            num_scalar_prefetch=1,
            grid=(batch, nt),
            in_specs=[cur(D_MODEL), prev(SWA_KV), cur(SWA_KV), prev(SWA_KV), cur(SWA_KV)],
            out_specs=cur(D_MODEL)),
        out_shape=jax.ShapeDtypeStruct((batch * seq, D_MODEL), BF16),
        compiler_params=pltpu.CompilerParams(
            dimension_semantics=("parallel", "parallel"), vmem_limit_bytes=VMEM_LIMIT),
        name="swa_attn",
    )(sinks, q, k, k, v, v)


def _post_mixer_kernel(*refs, layer, has_out_bias, next_qkv):
    it = iter(refs)
    mix_ref, h_ref, p_ref, wo_ref = next(it), next(it), next(it), next(it)
    bo_ref = next(it) if has_out_bias else None
    g1_all, b1_all, wup_ref, wdn_ref, g2_all, b2_all, wg_ref, bg_all, wp_ref = (
        next(it) for _ in range(9))
    g1_ref, b1_ref, g2_ref, b2_ref, bg_ref = (
        ref.at[layer:layer + 1, :] for ref in (g1_all, b1_all, g2_all, b2_all, bg_all))
    wqkv_ref, bqkv_ref = (next(it), next(it)) if next_qkv else (None, None)
    out_ref = next(it)
    q_ref, k_ref, v_ref = (next(it), next(it), next(it)) if next_qkv else (None, None, None)

    half = POST_TILE // 2
    rows = [slice(0, half), slice(half, POST_TILE)]

    def out_proj(r):
        y = DEEPNORM_ALPHA * h_ref[r, :] + _dot(mix_ref[r, :], wo_ref[...])
        return y + bo_ref[...] if has_out_bias else y

    def mlp(h1b, acc, chunks):
        for c in chunks:
            u = jnp.maximum(_dot(h1b, wup_ref[:, c * FF_TILE:(c + 1) * FF_TILE]), 0.0)
            acc = acc + _dot((u * u).astype(BF16), wdn_ref[c * FF_TILE:(c + 1) * FF_TILE, :])
        return acc

    def ple(r, h2):
        z = _dot(h2.astype(BF16), wg_ref[...]) + bg_ref[...]
        gate = 1.0 / (1.0 + jnp.exp(-z))
        out = h2 + gate * _dot(p_ref[r, :].astype(BF16), wp_ref[...])
        out_ref[r, :] = out
        if next_qkv:
            ob = out.astype(BF16)
            kcol, vcol = D_MODEL, D_MODEL + SWA_KV
            q = _dot(ob, wqkv_ref[:, :kcol]) + bqkv_ref[:, :kcol]
            q_ref[r, :] = (q * (SWA_HEAD_DIM ** -0.5 * LOG2E)).astype(BF16)
            k_ref[r, :] = (_dot(ob, wqkv_ref[:, kcol:vcol]) + bqkv_ref[:, kcol:vcol]).astype(BF16)
            v_ref[r, :] = (_dot(ob, wqkv_ref[:, vcol:]) + bqkv_ref[:, vcol:]).astype(BF16)

    n_ff = D_FF // FF_TILE
    y_a = out_proj(rows[0])
    y_b = out_proj(rows[1])
    h1_a = _layer_norm(y_a, g1_ref[...], b1_ref[...])
    h1b_a = h1_a.astype(BF16)
    acc_a = mlp(h1b_a, DEEPNORM_ALPHA * h1_a, range(0, 1))
    h1_b = _layer_norm(y_b, g1_ref[...], b1_ref[...])
    h1b_b = h1_b.astype(BF16)
    acc_a = mlp(h1b_a, acc_a, range(1, n_ff))
    acc_b = mlp(h1b_b, DEEPNORM_ALPHA * h1_b, range(0, 1))
    h2_a = _layer_norm(acc_a, g2_ref[...], b2_ref[...])
    acc_b = mlp(h1b_b, acc_b, range(1, n_ff))
    h2_b = _layer_norm(acc_b, g2_ref[...], b2_ref[...])
    ple(rows[0], h2_a)
    ple(rows[1], h2_b)


def _post_mixer(mix, h2d, p3, layer, weights, out_bias=None, qkv=None):
    t = h2d.shape[0]
    row = lambda w: pl.BlockSpec((POST_TILE, w), lambda i: (i, 0))
    wo, rest = weights[0], tuple(weights[1:])
    consts = (wo,) + ((out_bias,) if out_bias is not None else ()) + rest + (qkv or ())
    out_specs = [row(D_MODEL)]
    out_shape = [jax.ShapeDtypeStruct((t, D_MODEL), F32)]
    if qkv is not None:
        out_specs += [row(D_MODEL), row(SWA_KV), row(SWA_KV)]
        out_shape += [jax.ShapeDtypeStruct((t, D_MODEL), BF16),
                      jax.ShapeDtypeStruct((t, SWA_KV), BF16),
                      jax.ShapeDtypeStruct((t, SWA_KV), BF16)]
    return pl.pallas_call(
        functools.partial(_post_mixer_kernel, layer=layer, has_out_bias=out_bias is not None,
                          next_qkv=qkv is not None),
        grid=(t // POST_TILE,),
        in_specs=[row(D_MODEL), row(D_MODEL),
                  pl.BlockSpec((None, POST_TILE, PLE_DIM), lambda i: (layer, i, 0))]
                 + [_resident(w.shape, layer if w.ndim == 3 else None) for w in consts],
        out_specs=out_specs,
        out_shape=out_shape,
        compiler_params=pltpu.CompilerParams(
            dimension_semantics=("parallel",), vmem_limit_bytes=VMEM_LIMIT),
        name=f"post_mixer_{layer}",
    )(mix, h2d, p3, *consts)


def kernel(x, p, gla_w_in, gla_w_gk_up, gla_b_gk, gla_norm_g, gla_w_out, swa_w_qkv, swa_b_qkv,
           swa_sinks, swa_w_out, swa_b_out, mlp_w_up, mlp_w_down, ln1_g, ln1_b, ln2_g, ln2_b,
           ple_w_proj, ple_w_gate, ple_b_gate):
    batch, seq, d = x.shape
    t = batch * seq
    x2 = x.reshape(t, d)
    p3 = p.reshape(DEPTH, t, PLE_DIM)
    vec = lambda a: a.reshape(1, -1).astype(F32)

    flat = lambda w: w.reshape(-1, w.shape[-1])
    norm_g = vec(jnp.tile(gla_norm_g[0], GLA_HEADS))
    mix, w_up, w_down, w_gate, w_proj, w_out0, w_out1, w_qkv = _gla(
        x2, gla_w_in[0].astype(BF16), gla_w_gk_up[0], vec(gla_b_gk[0]), norm_g, seq,
        cast=[flat(mlp_w_up), flat(mlp_w_down), flat(ple_w_gate), flat(ple_w_proj),
              gla_w_out[0], swa_w_out[0], swa_w_qkv[0]])
    w_up, w_down = w_up.reshape(mlp_w_up.shape), w_down.reshape(mlp_w_down.shape)
    w_gate, w_proj = w_gate.reshape(ple_w_gate.shape), w_proj.reshape(ple_w_proj.shape)

    def post_weights(wo):
        return (wo, ln1_g, ln1_b, w_up, w_down, ln2_g, ln2_b, w_gate, ple_b_gate, w_proj)

    h, q, k, v = _post_mixer(mix, x2, p3, 0, post_weights(w_out0),
                             qkv=(w_qkv, vec(swa_b_qkv[0])))

    mix = _swa_attn(swa_sinks[0].astype(F32), q, k, v, batch, seq)
    (h,) = _post_mixer(mix, h, p3, 1, post_weights(w_out1), out_bias=vec(swa_b_out[0]))
    return h.reshape(batch, seq, d)
```

```python
import functools

import jax
import jax.numpy as jnp
from jax import lax
from jax.experimental import pallas as pl
from jax.experimental.pallas import tpu as pltpu

D_MODEL = 1024
DEPTH = 2
PLE_DIM = 256

GLA_HEADS = 4
GLA_DK = 128
GLA_DV = 256
GLA_LOWRANK = 16
GLA_TAU = 16.0
GLA_QK = GLA_HEADS * GLA_DK
GLA_V = GLA_HEADS * GLA_DV

SWA_HEAD_DIM = 64
SWA_Q_HEADS = 16
SWA_KV_HEADS = 4
SWA_GROUP = SWA_Q_HEADS // SWA_KV_HEADS
SWA_BLOCK = 128
SWA_Q_TILE = 1024
SWA_KV = SWA_KV_HEADS * SWA_HEAD_DIM

D_FF = 4 * D_MODEL
DEEPNORM_ALPHA = (2.0 * DEPTH) ** 0.25
LN_EPS = 1e-5
RMS_EPS = 1e-5
LOG2E = 1.4426950408889634

LANES = 128
SUBLANES = 8
BF16_SUBLANES = 16
GLA_CHUNK = 128
GLA_LEVELS = 7
ROW_TILE = 512
POST_TILE = 512
FF_TILE = 1024
VMEM_LIMIT = 60 * 1024 * 1024

BF16 = jnp.bfloat16
F32 = jnp.float32


def _dot(a, b):
    return jnp.dot(a, b, preferred_element_type=F32)


def _dot_nt(a, b):
    return lax.dot_general(a, b, (((1,), (1,)), ((), ())), preferred_element_type=F32)


def _resident(shape, layer=None):
    if layer is None:
        return pl.BlockSpec(shape, lambda *_: (0, 0), pipeline_mode=pl.Buffered(1))
    return pl.BlockSpec((None,) + tuple(shape[1:]), lambda *_: (layer, 0, 0),
                        pipeline_mode=pl.Buffered(1))


def _layer_norm(y, g, b):
    mu = jnp.mean(y, axis=-1, keepdims=True)
    yc = y - mu
    var = jnp.mean(yc * yc, axis=-1, keepdims=True)
    return yc * lax.rsqrt(var + LN_EPS) * g + b


def _gla_kernel(*refs, n_cast, tiles_per_seq):
    x_ref, w_ref, wup_ref, bgk_ref, ng_ref = refs[:5]
    cast_in = refs[5:5 + n_cast]
    o_ref = refs[5 + n_cast]
    cast_out = refs[6 + n_cast:6 + 2 * n_cast]
    q_s, k_s, v_s, g_s, b_s, state_ref = refs[6 + 2 * n_cast:]
    i = pl.program_id(0)

    @pl.when(i == 0)
    def _():
        q_s[1] = jnp.zeros(q_s.shape[1:], q_s.dtype)
        k_s[1] = jnp.zeros(k_s.shape[1:], k_s.dtype)
        v_s[1] = jnp.zeros(v_s.shape[1:], v_s.dtype)
        g_s[1] = jnp.zeros(g_s.shape[1:], g_s.dtype)
        b_s[1] = jnp.zeros(b_s.shape[1:], b_s.dtype)
        state_ref[...] = jnp.zeros_like(state_ref)

    for parity in range(2):
        @pl.when(i % 2 == parity)
        def _():
            _gla_step(x_ref, w_ref, wup_ref, bgk_ref, ng_ref, o_ref, q_s, k_s, v_s, g_s, b_s,
                      state_ref, cast_in, cast_out, wr=parity, tiles_per_seq=tiles_per_seq)


def _gla_step(x_ref, w_ref, wup_ref, bgk_ref, ng_ref, o_ref,
              q_s, k_s, v_s, g_s, b_s, state_ref, cast_in, cast_out, *, wr, tiles_per_seq):
    i = pl.program_id(0)
    rd = 1 - wr
    c_k, c_v, c_r, c_gl = GLA_QK, 2 * GLA_QK, 2 * GLA_QK + GLA_V, 2 * GLA_QK + 2 * GLA_V
    row = lax.broadcasted_iota(jnp.int32, (GLA_CHUNK, GLA_CHUNK), 0)
    col = lax.broadcasted_iota(jnp.int32, (GLA_CHUNK, GLA_CHUNK), 1)
    causal = col <= row
    tril2 = jnp.concatenate([causal.astype(BF16)] * 2, axis=1)
    heads = range(GLA_HEADS)
    ks = [slice(h * GLA_DK, (h + 1) * GLA_DK) for h in heads]
    vs = [slice(h * GLA_DV, (h + 1) * GLA_DV) for h in heads]

    fresh = (i - 1) % tiles_per_seq == 0
    state = [jnp.where(fresh, 0.0, state_ref[h]) for h in heads]

    n_chunks = ROW_TILE // GLA_CHUNK
    chunk_rows = [slice(c * GLA_CHUNK, (c + 1) * GLA_CHUNK) for c in range(n_chunks)]

    levels = [GLA_CHUNK >> l for l in range(GLA_LEVELS)]
    upper = [(row % blk) >= blk // 2 for blk in levels]
    pair_ok = [((row - row % blk) == (col - col % blk)) & up & ((col % blk) < blk // 2)
               for blk, up in zip(levels, upper)]
    diagonal = row == col
    sublane = lax.broadcasted_iota(jnp.int32, (GLA_CHUNK // SUBLANES, SUBLANES, GLA_DK), 1)

    def prepare(c):
        out = []
        for h in heads:
            b = b_s[rd, chunk_rows[c], ks[h]]
            b_last = b[GLA_CHUNK - 1:GLA_CHUNK, :]
            q = q_s[rd, chunk_rows[c], ks[h]]
            k = k_s[rd, chunk_rows[c], ks[h]]
            qk = jnp.sum(q.astype(F32) * k.astype(F32), axis=-1, keepdims=True)
            att = jnp.where(diagonal, qk, 0.0)
            b_tiles = b.reshape(GLA_CHUNK // SUBLANES, SUBLANES, GLA_DK)
            zero = jnp.zeros_like(q)
            for l in range(GLA_LEVELS):
                blk = levels[l]
                if blk >= SUBLANES:
                    rho = jnp.concatenate(
                        [jnp.broadcast_to(b[m + blk // 2 - 1:m + blk // 2, :], (blk, GLA_DK))
                         for m in range(0, GLA_CHUNK, blk)], axis=0)
                else:
                    rho = None
                    for start in range(0, SUBLANES, blk):
                        m = start + blk // 2 - 1
                        r = jnp.broadcast_to(b_tiles[:, m:m + 1, :], b_tiles.shape)
                        rho = r if rho is None else jnp.where(sublane >= start, r, rho)
                    rho = rho.reshape(GLA_CHUNK, GLA_DK)
                level_exp = -jnp.abs(b - rho)
                e = jnp.exp(level_exp).astype(BF16)
                kl_l = jnp.where(upper[l], zero, k * e)
                if blk >= 2 * BF16_SUBLANES:
                    runs = [slice(m + blk // 2, m + blk) for m in range(0, GLA_CHUNK, blk)]
                    qe = q * e
                    scores = _dot_nt(jnp.concatenate([qe[r] for r in runs], axis=0), kl_l)
                    if l > 0:
                        ok = jnp.concatenate([pair_ok[l][r] for r in runs], axis=0)
                        scores = jnp.where(ok, scores, 0.0)
                    gap = jnp.zeros((blk // 2, GLA_CHUNK), F32)
                    pieces = []
                    for n, r in enumerate(runs):
                        pieces += [gap, scores[n * (blk // 2):(n + 1) * (blk // 2)]]
                    att = att + jnp.concatenate(pieces, axis=0)
                else:
                    ql = jnp.where(upper[l], q * e, zero)
                    att = att + jnp.where(pair_ok[l], _dot_nt(ql, kl_l), 0.0)
            dec = jnp.broadcast_to(jnp.exp(b_last), (GLA_CHUNK, GLA_DK)).T
            out.append(dict(att=att.astype(BF16),
                            q_in=q * jnp.exp(b).astype(BF16),
                            kl=k * jnp.exp(b_last - b).astype(BF16),
                            dec=dec))
        return out

    half_v = GLA_V // 2
    xb = x_ref[...].astype(BF16)
    gl = _dot(xb, w_ref[:, c_gl:]).astype(BF16)
    prep = [prepare(0)]
    r0 = _dot(xb, w_ref[:, c_r:c_r + half_v])
    prep.append(prepare(1))
    r1 = _dot(xb, w_ref[:, c_r + half_v:c_gl])
    z = _dot(gl, wup_ref[...].astype(BF16)) + bgk_ref[...]
    prep.append(prepare(2))
    q_s[wr] = (_dot(xb, w_ref[:, :c_k]) * (GLA_DK ** -0.5)).astype(BF16)
    prep.append(prepare(3))
    k_s[wr] = _dot(xb, w_ref[:, c_k:c_v]).astype(BF16)
    for src, dst in zip(cast_in, cast_out):
        dst[...] = src[...].astype(BF16)

    upd = [[lax.dot_general(prep[c][h]["kl"], v_s[rd, chunk_rows[c], vs[h]],
                            (((0,), (0,)), ((), ())), preferred_element_type=F32)
            for h in heads] for c in range(n_chunks)]
    r = jnp.concatenate([r0, r1], axis=1)
    g_s[wr] = (r / (1.0 + jnp.exp(-r)) * ng_ref[...]).astype(BF16)
    states = [state]
    for c in range(n_chunks):
        states.append([jnp.concatenate(
            [states[c][h][:, j * GLA_CHUNK:(j + 1) * GLA_CHUNK] * prep[c][h]["dec"]
             for j in range(GLA_DV // GLA_CHUNK)], axis=1) + upd[c][h] for h in heads])
    v_s[wr, :, :half_v] = _dot(xb, w_ref[:, c_v:c_v + half_v]).astype(BF16)

    o = []
    for c in range(n_chunks):
        o.append([_dot(
            jnp.concatenate([prep[c][h]["att"], prep[c][h]["q_in"]], axis=1),
            jnp.concatenate([v_s[rd, chunk_rows[c], vs[h]], states[c][h].astype(BF16)], axis=0))
            for h in heads])
        if c == 1:
            v_s[wr, :, half_v:] = _dot(xb, w_ref[:, c_v + half_v:c_r]).astype(BF16)
    log_a = (jnp.minimum(z, 0.0) - jnp.log1p(jnp.exp(-jnp.abs(z)))) * (1.0 / GLA_TAU)
    hi = log_a.astype(BF16)
    lo = (log_a - hi.astype(F32)).astype(BF16)
    for c in range(n_chunks):
        b_s[wr, chunk_rows[c], :] = _dot(
            tril2, jnp.concatenate([hi[chunk_rows[c]], lo[chunk_rows[c]]], axis=0))
    for c in range(n_chunks):
        for h in heads:
            inv_rms = lax.rsqrt(jnp.mean(o[c][h] * o[c][h], axis=-1, keepdims=True) + RMS_EPS)
            gate = g_s[rd, chunk_rows[c], vs[h]].astype(F32)
            o_ref[chunk_rows[c], vs[h]] = (o[c][h] * inv_rms * gate).astype(BF16)
    state = states[n_chunks]
    for h in heads:
        state_ref[h] = state[h]


def _gla(x2, w_in, wup, bgk, ng, seq, cast):
    t = x2.shape[0]
    n = t // ROW_TILE
    assert ROW_TILE // GLA_CHUNK == 4 and seq % ROW_TILE == 0

    def cast_spec(w):
        rows = max(BF16_SUBLANES, w.shape[0] // n)
        last = w.shape[0] // rows - 1
        return pl.BlockSpec((rows, w.shape[1]), lambda i: (jnp.minimum(i, last), 0))

    cast_specs = [cast_spec(w) for w in cast]
    return pl.pallas_call(
        functools.partial(_gla_kernel, n_cast=len(cast), tiles_per_seq=seq // ROW_TILE),
        grid=(n + 1,),
        in_specs=[pl.BlockSpec((ROW_TILE, D_MODEL), lambda i: (jnp.minimum(i, n - 1), 0)),
                  _resident(w_in.shape), _resident(wup.shape), _resident(bgk.shape),
                  _resident(ng.shape)] + cast_specs,
        out_specs=[pl.BlockSpec((ROW_TILE, GLA_V), lambda i: (jnp.maximum(i - 1, 0), 0))]
                  + cast_specs,
        out_shape=[jax.ShapeDtypeStruct((t, GLA_V), BF16)]
                  + [jax.ShapeDtypeStruct(w.shape, BF16) for w in cast],
        scratch_shapes=[pltpu.VMEM((2, ROW_TILE, GLA_QK), BF16),
                        pltpu.VMEM((2, ROW_TILE, GLA_QK), BF16),
                        pltpu.VMEM((2, ROW_TILE, GLA_V), BF16),
                        pltpu.VMEM((2, ROW_TILE, GLA_V), BF16),
                        pltpu.VMEM((2, ROW_TILE, GLA_QK), F32),
                        pltpu.VMEM((GLA_HEADS, GLA_DK, GLA_DV), F32)],
        compiler_params=pltpu.CompilerParams(
            dimension_semantics=("arbitrary",), vmem_limit_bytes=VMEM_LIMIT),
        name="gla",
    )(x2, w_in, wup, bgk, ng, *cast)


def _swa_attn_kernel(sink_ref, q_ref, kp_ref, kc_ref, vp_ref, vc_ref, o_ref):
    has_prev = (pl.program_id(1) > 0).astype(BF16)
    k_all = jnp.concatenate([kp_ref[...] * has_prev, kc_ref[...]], axis=0)
    v_all = jnp.concatenate([vp_ref[...] * has_prev, vc_ref[...]], axis=0)
    grp = SWA_GROUP * SWA_BLOCK
    kj = lax.broadcasted_iota(jnp.int32, (SWA_BLOCK, grp), 0)
    qi = lax.broadcasted_iota(jnp.int32, (SWA_BLOCK, grp), 1) % SWA_BLOCK
    from_prev = kj > qi
    lane = lax.broadcasted_iota(jnp.int32, (SWA_BLOCK, LANES), 1)
    even_lanes = lane < SWA_HEAD_DIM
    one = jnp.ones((2 * SWA_BLOCK, SWA_HEAD_DIM), BF16)
    pairs_per_kv = SWA_GROUP // 2
    sink_rows = [jnp.concatenate(
        [jnp.full((1, SWA_BLOCK), sink_ref[kv * SWA_GROUP + g] * LOG2E, F32)
         for g in range(SWA_GROUP)], axis=1) for kv in range(SWA_KV_HEADS)]
    units = [(blk, kv) for blk in range(SWA_Q_TILE // SWA_BLOCK) for kv in range(SWA_KV_HEADS)]
    band = lambda blk: slice(blk * SWA_BLOCK, (blk + 2) * SWA_BLOCK)
    rows = lambda blk: slice(blk * SWA_BLOCK, (blk + 1) * SWA_BLOCK)
    heads = lambda kv: slice(kv * SWA_HEAD_DIM, (kv + 1) * SWA_HEAD_DIM)
    scores, maxes = [], []
    for blk, kv in units:
        kh = k_all[band(blk), heads(kv)]
        k2 = jnp.concatenate([kh, kh], axis=1)
        q_heads = []
        for j in range(pairs_per_kv):
            hp = kv * pairs_per_kv + j
            q_pair = q_ref[rows(blk), hp * LANES:(hp + 1) * LANES]
            q_heads += [jnp.where(even_lanes, q_pair, jnp.zeros_like(q_pair)),
                        jnp.where(even_lanes, jnp.zeros_like(q_pair), q_pair)]
        s2 = _dot_nt(k2, jnp.concatenate(q_heads, axis=0))
        s = jnp.where(from_prev, s2[:SWA_BLOCK], s2[SWA_BLOCK:])
        scores.append(s)
        maxes.append(jnp.maximum(jnp.max(s, axis=0, keepdims=True), sink_rows[kv]))
    probs = []
    for u in range(len(units)):
        p = jnp.exp2(scores[u] - maxes[u])
        probs.append(jnp.concatenate([jnp.where(from_prev, p, 0.0),
                                      jnp.where(from_prev, 0.0, p)], axis=0).astype(BF16))
    outs = []
    for u, (blk, kv) in enumerate(units):
        v_ext = jnp.concatenate([v_all[band(blk), heads(kv)], one], axis=1)
        pv = lax.dot_general(v_ext, probs[u], (((0,), (0,)), ((), ())),
                             preferred_element_type=F32)
        den = pv[SWA_HEAD_DIM:] + jnp.exp2(sink_rows[kv] - maxes[u])
        outs.append(pv[:SWA_HEAD_DIM] / den)
    for u, (blk, kv) in enumerate(units):
        for j in range(pairs_per_kv):
            hp = kv * pairs_per_kv + j
            g0 = j * 2 * SWA_BLOCK
            pair = jnp.concatenate([outs[u][:, g0:g0 + SWA_BLOCK],
                                    outs[u][:, g0 + SWA_BLOCK:g0 + 2 * SWA_BLOCK]], axis=0)
            o_ref[rows(blk), hp * LANES:(hp + 1) * LANES] = pair.T.astype(BF16)


def _swa_attn(sinks, q, k, v, batch, seq):
    nt = seq // SWA_Q_TILE
    per = SWA_Q_TILE // SWA_BLOCK
    cur = lambda w: pl.BlockSpec((SWA_Q_TILE, w), lambda b, n, *_: (b * nt + n, 0))
    prev = lambda w: pl.BlockSpec(
        (SWA_BLOCK, w), lambda b, n, *_: (b * nt * per + jnp.maximum(n * per - 1, 0), 0))
    return pl.pallas_call(
        _swa_attn_kernel,
        grid_spec=pltpu.PrefetchScalarGridSpec(
            num_scalar_prefetch=1,
            grid=(batch, nt),
            in_specs=[cur(D_MODEL), prev(SWA_KV), cur(SWA_KV), prev(SWA_KV), cur(SWA_KV)],
            out_specs=cur(D_MODEL)),
        out_shape=jax.ShapeDtypeStruct((batch * seq, D_MODEL), BF16),
        compiler_params=pltpu.CompilerParams(
            dimension_semantics=("parallel", "parallel"), vmem_limit_bytes=VMEM_LIMIT),
        name="swa_attn",
    )(sinks, q, k, k, v, v)


def _post_mixer_kernel(*refs, layer, has_out_bias, next_qkv):
    it = iter(refs)
    mix_ref, h_ref, p_ref, wo_ref = next(it), next(it), next(it), next(it)
    bo_ref = next(it) if has_out_bias else None
    g1_all, b1_all, wup_ref, wdn_ref, g2_all, b2_all, wg_ref, bg_all, wp_ref = (
        next(it) for _ in range(9))
    g1_ref, b1_ref, g2_ref, b2_ref, bg_ref = (
        ref.at[layer:layer + 1, :] for ref in (g1_all, b1_all, g2_all, b2_all, bg_all))
    wqkv_ref, bqkv_ref = (next(it), next(it)) if next_qkv else (None, None)
    out_ref = next(it)
    q_ref, k_ref, v_ref = (next(it), next(it), next(it)) if next_qkv else (None, None, None)

    half = POST_TILE // 2
    rows = [slice(0, half), slice(half, POST_TILE)]

    def out_proj(r):
        y = DEEPNORM_ALPHA * h_ref[r, :] + _dot(mix_ref[r, :], wo_ref[...])
        return y + bo_ref[...] if has_out_bias else y

    def mlp(h1b, acc, chunks):
        for c in chunks:
            u = jnp.maximum(_dot(h1b, wup_ref[:, c * FF_TILE:(c + 1) * FF_TILE]), 0.0)
            acc = acc + _dot((u * u).astype(BF16), wdn_ref[c * FF_TILE:(c + 1) * FF_TILE, :])
        return acc

    def ple(r, h2):
        z = _dot(h2.astype(BF16), wg_ref[...]) + bg_ref[...]
        gate = 1.0 / (1.0 + jnp.exp(-z))
        out = h2 + gate * _dot(p_ref[r, :].astype(BF16), wp_ref[...])
        out_ref[r, :] = out
        if next_qkv:
            ob = out.astype(BF16)
            kcol, vcol = D_MODEL, D_MODEL + SWA_KV
            q = _dot(ob, wqkv_ref[:, :kcol]) + bqkv_ref[:, :kcol]
            q_ref[r, :] = (q * (SWA_HEAD_DIM ** -0.5 * LOG2E)).astype(BF16)
            k_ref[r, :] = (_dot(ob, wqkv_ref[:, kcol:vcol]) + bqkv_ref[:, kcol:vcol]).astype(BF16)
            v_ref[r, :] = (_dot(ob, wqkv_ref[:, vcol:]) + bqkv_ref[:, vcol:]).astype(BF16)

    n_ff = D_FF // FF_TILE
    y_a = out_proj(rows[0])
    y_b = out_proj(rows[1])
    h1_a = _layer_norm(y_a, g1_ref[...], b1_ref[...])
    h1b_a = h1_a.astype(BF16)
    acc_a = mlp(h1b_a, DEEPNORM_ALPHA * h1_a, range(0, 1))
    h1_b = _layer_norm(y_b, g1_ref[...], b1_ref[...])
    h1b_b = h1_b.astype(BF16)
    acc_a = mlp(h1b_a, acc_a, range(1, n_ff))
    acc_b = mlp(h1b_b, DEEPNORM_ALPHA * h1_b, range(0, 1))
    h2_a = _layer_norm(acc_a, g2_ref[...], b2_ref[...])
    acc_b = mlp(h1b_b, acc_b, range(1, n_ff))
    h2_b = _layer_norm(acc_b, g2_ref[...], b2_ref[...])
    ple(rows[0], h2_a)
    ple(rows[1], h2_b)


def _post_mixer(mix, h2d, p3, layer, weights, out_bias=None, qkv=None):
    t = h2d.shape[0]
    row = lambda w: pl.BlockSpec((POST_TILE, w), lambda i: (i, 0))
    wo, rest = weights[0], tuple(weights[1:])
    consts = (wo,) + ((out_bias,) if out_bias is not None else ()) + rest + (qkv or ())
    out_specs = [row(D_MODEL)]
    out_shape = [jax.ShapeDtypeStruct((t, D_MODEL), F32)]
    if qkv is not None:
        out_specs += [row(D_MODEL), row(SWA_KV), row(SWA_KV)]
        out_shape += [jax.ShapeDtypeStruct((t, D_MODEL), BF16),
                      jax.ShapeDtypeStruct((t, SWA_KV), BF16),
                      jax.ShapeDtypeStruct((t, SWA_KV), BF16)]
    return pl.pallas_call(
        functools.partial(_post_mixer_kernel, layer=layer, has_out_bias=out_bias is not None,
                          next_qkv=qkv is not None),
        grid=(t // POST_TILE,),
        in_specs=[row(D_MODEL), row(D_MODEL),
                  pl.BlockSpec((None, POST_TILE, PLE_DIM), lambda i: (layer, i, 0))]
                 + [_resident(w.shape, layer if w.ndim == 3 else None) for w in consts],
        out_specs=out_specs,
        out_shape=out_shape,
        compiler_params=pltpu.CompilerParams(
            dimension_semantics=("parallel",), vmem_limit_bytes=VMEM_LIMIT),
        name=f"post_mixer_{layer}",
    )(mix, h2d, p3, *consts)


def kernel(x, p, gla_w_in, gla_w_gk_up, gla_b_gk, gla_norm_g, gla_w_out, swa_w_qkv, swa_b_qkv,
           swa_sinks, swa_w_out, swa_b_out, mlp_w_up, mlp_w_down, ln1_g, ln1_b, ln2_g, ln2_b,
           ple_w_proj, ple_w_gate, ple_b_gate):
    batch, seq, d = x.shape
    t = batch * seq
    x2 = x.reshape(t, d)
    p3 = p.reshape(DEPTH, t, PLE_DIM)
    vec = lambda a: a.reshape(1, -1).astype(F32)

    flat = lambda w: w.reshape(-1, w.shape[-1])
    norm_g = vec(jnp.tile(gla_norm_g[0], GLA_HEADS))
    mix, w_up, w_down, w_gate, w_proj, w_out0, w_out1, w_qkv = _gla(
        x2, gla_w_in[0].astype(BF16), gla_w_gk_up[0], vec(gla_b_gk[0]), norm_g, seq,
        cast=[flat(mlp_w_up), flat(mlp_w_down), flat(ple_w_gate), flat(ple_w_proj),
              gla_w_out[0], swa_w_out[0], swa_w_qkv[0]])
    w_up, w_down = w_up.reshape(mlp_w_up.shape), w_down.reshape(mlp_w_down.shape)
    w_gate, w_proj = w_gate.reshape(ple_w_gate.shape), w_proj.reshape(ple_w_proj.shape)

    def post_weights(wo):
        return (wo, ln1_g, ln1_b, w_up, w_down, ln2_g, ln2_b, w_gate, ple_b_gate, w_proj)

    h, q, k, v = _post_mixer(mix, x2, p3, 0, post_weights(w_out0),
                             qkv=(w_qkv, vec(swa_b_qkv[0])))

    mix = _swa_attn(swa_sinks[0].astype(F32), q, k, v, batch, seq)
    (h,) = _post_mixer(mix, h, p3, 1, post_weights(w_out1), out_bias=vec(swa_b_out[0]))
    return h.reshape(batch, seq, d)
```

```python
import functools

import jax
import jax.numpy as jnp
from jax import lax
from jax.experimental import pallas as pl
from jax.experimental.pallas import tpu as pltpu

D_MODEL = 1024
DEPTH = 2
PLE_DIM = 256

GLA_HEADS = 4
GLA_DK = 128
GLA_DV = 256
GLA_LOWRANK = 16
GLA_TAU = 16.0
GLA_QK = GLA_HEADS * GLA_DK
GLA_V = GLA_HEADS * GLA_DV

SWA_HEAD_DIM = 64
SWA_Q_HEADS = 16
SWA_KV_HEADS = 4
SWA_GROUP = SWA_Q_HEADS // SWA_KV_HEADS
SWA_BLOCK = 128
SWA_Q_TILE = 1024
SWA_KV = SWA_KV_HEADS * SWA_HEAD_DIM

D_FF = 4 * D_MODEL
DEEPNORM_ALPHA = (2.0 * DEPTH) ** 0.25
LN_EPS = 1e-5
RMS_EPS = 1e-5
LOG2E = 1.4426950408889634

LANES = 128
SUBLANES = 8
BF16_SUBLANES = 16
GLA_CHUNK = 128
GLA_LEVELS = 7
GLA_DIRECT_LEVELS = 1
ROW_TILE = 512
POST_TILE = 512
FF_TILE = 1024
VMEM_LIMIT = 60 * 1024 * 1024

BF16 = jnp.bfloat16
F32 = jnp.float32


def _dot(a, b):
    return jnp.dot(a, b, preferred_element_type=F32)


def _dot_nt(a, b):
    return lax.dot_general(a, b, (((1,), (1,)), ((), ())), preferred_element_type=F32)


def _resident(shape, layer=None):
    if layer is None:
        return pl.BlockSpec(shape, lambda *_: (0, 0), pipeline_mode=pl.Buffered(1))
    return pl.BlockSpec((None,) + tuple(shape[1:]), lambda *_: (layer, 0, 0),
                        pipeline_mode=pl.Buffered(1))


def _layer_norm(y, g, b):
    mu = jnp.mean(y, axis=-1, keepdims=True)
    yc = y - mu
    var = jnp.mean(yc * yc, axis=-1, keepdims=True)
    return yc * lax.rsqrt(var + LN_EPS) * g + b


def _gla_kernel(*refs, n_cast, tiles_per_seq):
    x_ref, w_ref, wup_ref, bgk_ref, ng_ref = refs[:5]
    cast_in = refs[5:5 + n_cast]
    o_ref = refs[5 + n_cast]
    cast_out = refs[6 + n_cast:6 + 2 * n_cast]
    q_s, k_s, v_s, g_s, b_s, state_ref = refs[6 + 2 * n_cast:]
    i = pl.program_id(0)

    @pl.when(i == 0)
    def _():
        q_s[1] = jnp.zeros(q_s.shape[1:], q_s.dtype)
        k_s[1] = jnp.zeros(k_s.shape[1:], k_s.dtype)
        v_s[1] = jnp.zeros(v_s.shape[1:], v_s.dtype)
        g_s[1] = jnp.zeros(g_s.shape[1:], g_s.dtype)
        b_s[1] = jnp.zeros(b_s.shape[1:], b_s.dtype)
        state_ref[...] = jnp.zeros_like(state_ref)

    for parity in range(2):
        @pl.when(i % 2 == parity)
        def _():
            _gla_step(x_ref, w_ref, wup_ref, bgk_ref, ng_ref, o_ref, q_s, k_s, v_s, g_s, b_s,
                      state_ref, cast_in, cast_out, wr=parity, tiles_per_seq=tiles_per_seq)


def _gla_step(x_ref, w_ref, wup_ref, bgk_ref, ng_ref, o_ref,
              q_s, k_s, v_s, g_s, b_s, state_ref, cast_in, cast_out, *, wr, tiles_per_seq):
    i = pl.program_id(0)
    rd = 1 - wr
    c_k, c_v, c_r, c_gl = GLA_QK, 2 * GLA_QK, 2 * GLA_QK + GLA_V, 2 * GLA_QK + 2 * GLA_V
    row = lax.broadcasted_iota(jnp.int32, (GLA_CHUNK, GLA_CHUNK), 0)
    col = lax.broadcasted_iota(jnp.int32, (GLA_CHUNK, GLA_CHUNK), 1)
    causal = col <= row
    tril2 = jnp.concatenate([causal.astype(BF16)] * 2, axis=1)
    heads = range(GLA_HEADS)
    ks = [slice(h * GLA_DK, (h + 1) * GLA_DK) for h in heads]
    vs = [slice(h * GLA_DV, (h + 1) * GLA_DV) for h in heads]

    fresh = (i - 1) % tiles_per_seq == 0
    state = [jnp.where(fresh, 0.0, state_ref[h]) for h in heads]

    n_chunks = ROW_TILE // GLA_CHUNK
    chunk_rows = [slice(c * GLA_CHUNK, (c + 1) * GLA_CHUNK) for c in range(n_chunks)]

    levels = [GLA_CHUNK >> l for l in range(GLA_LEVELS)]
    upper = [(row % blk) >= blk // 2 for blk in levels]
    pair_ok = [((row - row % blk) == (col - col % blk)) & up & ((col % blk) < blk // 2)
               for blk, up in zip(levels, upper)]
    diagonal = row == col
    sublane = lax.broadcasted_iota(jnp.int32, (GLA_CHUNK // SUBLANES, SUBLANES, GLA_DK), 1)

    def prepare(c):
        out = []
        for h in heads:
            b = b_s[rd, chunk_rows[c], ks[h]]
            b_last = b[GLA_CHUNK - 1:GLA_CHUNK, :]
            q = q_s[rd, chunk_rows[c], ks[h]]
            k = k_s[rd, chunk_rows[c], ks[h]]
            q32, k32 = q.astype(F32), k.astype(F32)
            qk = jnp.sum(q32 * k32, axis=-1, keepdims=True)
            att = jnp.where(diagonal, qk, 0.0)
            for blk in levels[GLA_LEVELS - GLA_DIRECT_LEVELS:]:
                pos = row % blk
                for d in range(1, blk):
                    pair_rows = (pos >= blk // 2) & (pos - d >= 0) & (pos - d < blk // 2)
                    decay = jnp.where(pair_rows, b - pltpu.roll(b, d, 0), 0.0)
                    pair = jnp.sum(q32 * pltpu.roll(k32, d, 0) * jnp.exp(decay),
                                   axis=-1, keepdims=True)
                    att = att + jnp.where(pair_rows & (col == row - d), pair, 0.0)
            b_tiles = b.reshape(GLA_CHUNK // SUBLANES, SUBLANES, GLA_DK)
            zero = jnp.zeros_like(q)
            for l in range(GLA_LEVELS - GLA_DIRECT_LEVELS):
                blk = levels[l]
                if blk >= SUBLANES:
                    rho = jnp.concatenate(
                        [jnp.broadcast_to(b[m + blk // 2 - 1:m + blk // 2, :], (blk, GLA_DK))
                         for m in range(0, GLA_CHUNK, blk)], axis=0)
                else:
                    rho = None
                    for start in range(0, SUBLANES, blk):
                        m = start + blk // 2 - 1
                        r = jnp.broadcast_to(b_tiles[:, m:m + 1, :], b_tiles.shape)
                        rho = r if rho is None else jnp.where(sublane >= start, r, rho)
                    rho = rho.reshape(GLA_CHUNK, GLA_DK)
                level_exp = -jnp.abs(b - rho)
                e = jnp.exp(level_exp).astype(BF16)
                kl_l = jnp.where(upper[l], zero, k * e)
                if blk >= 2 * BF16_SUBLANES:
                    runs = [slice(m + blk // 2, m + blk) for m in range(0, GLA_CHUNK, blk)]
                    qe = q * e
                    scores = _dot_nt(jnp.concatenate([qe[r] for r in runs], axis=0), kl_l)
                    if l > 0:
                        ok = jnp.concatenate([pair_ok[l][r] for r in runs], axis=0)
                        scores = jnp.where(ok, scores, 0.0)
                    gap = jnp.zeros((blk // 2, GLA_CHUNK), F32)
                    pieces = []
                    for n, r in enumerate(runs):
                        pieces += [gap, scores[n * (blk // 2):(n + 1) * (blk // 2)]]
                    att = att + jnp.concatenate(pieces, axis=0)
                else:
                    ql = jnp.where(upper[l], q * e, zero)
                    att = att + jnp.where(pair_ok[l], _dot_nt(ql, kl_l), 0.0)
            dec = jnp.broadcast_to(jnp.exp(b_last), (GLA_CHUNK, GLA_DK)).T
            out.append(dict(att=att.astype(BF16),
                            q_in=q * jnp.exp(b).astype(BF16),
                            kl=k * jnp.exp(b_last - b).astype(BF16),
                            dec=dec))
        return out

    half_v = GLA_V // 2
    xb = x_ref[...].astype(BF16)
    gl = _dot(xb, w_ref[:, c_gl:]).astype(BF16)
    prep = [prepare(0)]
    r0 = _dot(xb, w_ref[:, c_r:c_r + half_v])
    prep.append(prepare(1))
    r1 = _dot(xb, w_ref[:, c_r + half_v:c_gl])
    z = _dot(gl, wup_ref[...].astype(BF16)) + bgk_ref[...]
    prep.append(prepare(2))
    q_s[wr] = (_dot(xb, w_ref[:, :c_k]) * (GLA_DK ** -0.5)).astype(BF16)
    prep.append(prepare(3))
    k_s[wr] = _dot(xb, w_ref[:, c_k:c_v]).astype(BF16)
    for src, dst in zip(cast_in, cast_out):
        dst[...] = src[...].astype(BF16)

    upd = [[lax.dot_general(prep[c][h]["kl"], v_s[rd, chunk_rows[c], vs[h]],
                            (((0,), (0,)), ((), ())), preferred_element_type=F32)
            for h in heads] for c in range(n_chunks)]
    r = jnp.concatenate([r0, r1], axis=1)
    g_s[wr] = (r / (1.0 + jnp.exp(-r)) * ng_ref[...]).astype(BF16)
    states = [state]
    for c in range(n_chunks):
        states.append([jnp.concatenate(
            [states[c][h][:, j * GLA_CHUNK:(j + 1) * GLA_CHUNK] * prep[c][h]["dec"]
             for j in range(GLA_DV // GLA_CHUNK)], axis=1) + upd[c][h] for h in heads])
    v_s[wr, :, :half_v] = _dot(xb, w_ref[:, c_v:c_v + half_v]).astype(BF16)

    o = []
    for c in range(n_chunks):
        o.append([_dot(
            jnp.concatenate([prep[c][h]["att"], prep[c][h]["q_in"]], axis=1),
            jnp.concatenate([v_s[rd, chunk_rows[c], vs[h]], states[c][h].astype(BF16)], axis=0))
            for h in heads])
        if c == 1:
            v_s[wr, :, half_v:] = _dot(xb, w_ref[:, c_v + half_v:c_r]).astype(BF16)
    log_a = (jnp.minimum(z, 0.0) - jnp.log1p(jnp.exp(-jnp.abs(z)))) * (1.0 / GLA_TAU)
    hi = log_a.astype(BF16)
    lo = (log_a - hi.astype(F32)).astype(BF16)
    for c in range(n_chunks):
        b_s[wr, chunk_rows[c], :] = _dot(
            tril2, jnp.concatenate([hi[chunk_rows[c]], lo[chunk_rows[c]]], axis=0))
    for c in range(n_chunks):
        for h in heads:
            inv_rms = lax.rsqrt(jnp.mean(o[c][h] * o[c][h], axis=-1, keepdims=True) + RMS_EPS)
            gate = g_s[rd, chunk_rows[c], vs[h]].astype(F32)
            o_ref[chunk_rows[c], vs[h]] = (o[c][h] * inv_rms * gate).astype(BF16)
    state = states[n_chunks]
    for h in heads:
        state_ref[h] = state[h]


def _gla(x2, w_in, wup, bgk, ng, seq, cast):
    t = x2.shape[0]
    n = t // ROW_TILE
    assert ROW_TILE // GLA_CHUNK == 4 and seq % ROW_TILE == 0

    def cast_spec(w):
        rows = max(BF16_SUBLANES, w.shape[0] // n)
        last = w.shape[0] // rows - 1
        return pl.BlockSpec((rows, w.shape[1]), lambda i: (jnp.minimum(i, last), 0))

    cast_specs = [cast_spec(w) for w in cast]
    return pl.pallas_call(
        functools.partial(_gla_kernel, n_cast=len(cast), tiles_per_seq=seq // ROW_TILE),
        grid=(n + 1,),
        in_specs=[pl.BlockSpec((ROW_TILE, D_MODEL), lambda i: (jnp.minimum(i, n - 1), 0)),
                  _resident(w_in.shape), _resident(wup.shape), _resident(bgk.shape),
                  _resident(ng.shape)] + cast_specs,
        out_specs=[pl.BlockSpec((ROW_TILE, GLA_V), lambda i: (jnp.maximum(i - 1, 0), 0))]
                  + cast_specs,
        out_shape=[jax.ShapeDtypeStruct((t, GLA_V), BF16)]
                  + [jax.ShapeDtypeStruct(w.shape, BF16) for w in cast],
        scratch_shapes=[pltpu.VMEM((2, ROW_TILE, GLA_QK), BF16),
                        pltpu.VMEM((2, ROW_TILE, GLA_QK), BF16),
                        pltpu.VMEM((2, ROW_TILE, GLA_V), BF16),
                        pltpu.VMEM((2, ROW_TILE, GLA_V), BF16),
                        pltpu.VMEM((2, ROW_TILE, GLA_QK), F32),
                        pltpu.VMEM((GLA_HEADS, GLA_DK, GLA_DV), F32)],
        compiler_params=pltpu.CompilerParams(
            dimension_semantics=("arbitrary",), vmem_limit_bytes=VMEM_LIMIT),
        name="gla",
    )(x2, w_in, wup, bgk, ng, *cast)


def _swa_attn_kernel(sink_ref, q_ref, kp_ref, kc_ref, vp_ref, vc_ref, o_ref):
    has_prev = (pl.program_id(1) > 0).astype(BF16)
    k_all = jnp.concatenate([kp_ref[...] * has_prev, kc_ref[...]], axis=0)
    v_all = jnp.concatenate([vp_ref[...] * has_prev, vc_ref[...]], axis=0)
    grp = SWA_GROUP * SWA_BLOCK
    kj = lax.broadcasted_iota(jnp.int32, (SWA_BLOCK, grp), 0)
    qi = lax.broadcasted_iota(jnp.int32, (SWA_BLOCK, grp), 1) % SWA_BLOCK
    from_prev = kj > qi
    lane = lax.broadcasted_iota(jnp.int32, (SWA_BLOCK, LANES), 1)
    even_lanes = lane < SWA_HEAD_DIM
    one = jnp.ones((2 * SWA_BLOCK, SWA_HEAD_DIM), BF16)
    pairs_per_kv = SWA_GROUP // 2
    sink_rows = [jnp.concatenate(
        [jnp.full((1, SWA_BLOCK), sink_ref[kv * SWA_GROUP + g] * LOG2E, F32)
         for g in range(SWA_GROUP)], axis=1) for kv in range(SWA_KV_HEADS)]
    units = [(blk, kv) for blk in range(SWA_Q_TILE // SWA_BLOCK) for kv in range(SWA_KV_HEADS)]
    band = lambda blk: slice(blk * SWA_BLOCK, (blk + 2) * SWA_BLOCK)
    rows = lambda blk: slice(blk * SWA_BLOCK, (blk + 1) * SWA_BLOCK)
    heads = lambda kv: slice(kv * SWA_HEAD_DIM, (kv + 1) * SWA_HEAD_DIM)
    scores, maxes = [], []
    for blk, kv in units:
        kh = k_all[band(blk), heads(kv)]
        k2 = jnp.concatenate([kh, kh], axis=1)
        q_heads = []
        for j in range(pairs_per_kv):
            hp = kv * pairs_per_kv + j
            q_pair = q_ref[rows(blk), hp * LANES:(hp + 1) * LANES]
            q_heads += [jnp.where(even_lanes, q_pair, jnp.zeros_like(q_pair)),
                        jnp.where(even_lanes, jnp.zeros_like(q_pair), q_pair)]
        s2 = _dot_nt(k2, jnp.concatenate(q_heads, axis=0))
        s = jnp.where(from_prev, s2[:SWA_BLOCK], s2[SWA_BLOCK:])
        scores.append(s)
        maxes.append(jnp.maximum(jnp.max(s, axis=0, keepdims=True), sink_rows[kv]))
    probs = []
    for u in range(len(units)):
        p = jnp.exp2(scores[u] - maxes[u])
        probs.append(jnp.concatenate([jnp.where(from_prev, p, 0.0),
                                      jnp.where(from_prev, 0.0, p)], axis=0).astype(BF16))
    outs = []
    for u, (blk, kv) in enumerate(units):
        v_ext = jnp.concatenate([v_all[band(blk), heads(kv)], one], axis=1)
        pv = lax.dot_general(v_ext, probs[u], (((0,), (0,)), ((), ())),
                             preferred_element_type=F32)
        den = pv[SWA_HEAD_DIM:] + jnp.exp2(sink_rows[kv] - maxes[u])
        outs.append(pv[:SWA_HEAD_DIM] / den)
    for u, (blk, kv) in enumerate(units):
        for j in range(pairs_per_kv):
            hp = kv * pairs_per_kv + j
            g0 = j * 2 * SWA_BLOCK
            pair = jnp.concatenate([outs[u][:, g0:g0 + SWA_BLOCK],
                                    outs[u][:, g0 + SWA_BLOCK:g0 + 2 * SWA_BLOCK]], axis=0)
            o_ref[rows(blk), hp * LANES:(hp + 1) * LANES] = pair.T.astype(BF16)


def _swa_attn(sinks, q, k, v, batch, seq):
    nt = seq // SWA_Q_TILE
    per = SWA_Q_TILE // SWA_BLOCK
    cur = lambda w: pl.BlockSpec((SWA_Q_TILE, w), lambda b, n, *_: (b * nt + n, 0))
    prev = lambda w: pl.BlockSpec(
        (SWA_BLOCK, w), lambda b, n, *_: (b * nt * per + jnp.maximum(n * per - 1, 0), 0))
    return pl.pallas_call(
        _swa_attn_kernel,
        grid_spec=pltpu.PrefetchScalarGridSpec(
            num_scalar_prefetch=1,
            grid=(batch, nt),
            in_specs=[cur(D_MODEL), prev(SWA_KV), cur(SWA_KV), prev(SWA_KV), cur(SWA_KV)],
            out_specs=cur(D_MODEL)),
        out_shape=jax.ShapeDtypeStruct((batch * seq, D_MODEL), BF16),
        compiler_params=pltpu.CompilerParams(
            dimension_semantics=("parallel", "parallel"), vmem_limit_bytes=VMEM_LIMIT),
        name="swa_attn",
    )(sinks, q, k, k, v, v)


def _post_mixer_kernel(*refs, layer, has_out_bias, next_qkv):
    it = iter(refs)
    mix_ref, h_ref, p_ref, wo_ref = next(it), next(it), next(it), next(it)
    bo_ref = next(it) if has_out_bias else None
    g1_all, b1_all, wup_ref, wdn_ref, g2_all, b2_all, wg_ref, bg_all, wp_ref = (
        next(it) for _ in range(9))
    g1_ref, b1_ref, g2_ref, b2_ref, bg_ref = (
        ref.at[layer:layer + 1, :] for ref in (g1_all, b1_all, g2_all, b2_all, bg_all))
    wqkv_ref, bqkv_ref = (next(it), next(it)) if next_qkv else (None, None)
    out_ref = next(it)
    q_ref, k_ref, v_ref = (next(it), next(it), next(it)) if next_qkv else (None, None, None)

    half = POST_TILE // 2
    rows = [slice(0, half), slice(half, POST_TILE)]

    def out_proj(r):
        y = DEEPNORM_ALPHA * h_ref[r, :] + _dot(mix_ref[r, :], wo_ref[...])
        return y + bo_ref[...] if has_out_bias else y

    def mlp(h1b, acc, chunks):
        for c in chunks:
            u = jnp.maximum(_dot(h1b, wup_ref[:, c * FF_TILE:(c + 1) * FF_TILE]), 0.0)
            acc = acc + _dot((u * u).astype(BF16), wdn_ref[c * FF_TILE:(c + 1) * FF_TILE, :])
        return acc

    def ple(r, h2):
        z = _dot(h2.astype(BF16), wg_ref[...]) + bg_ref[...]
        gate = 1.0 / (1.0 + jnp.exp(-z))
        out = h2 + gate * _dot(p_ref[r, :].astype(BF16), wp_ref[...])
        out_ref[r, :] = out
        if next_qkv:
            ob = out.astype(BF16)
            kcol, vcol = D_MODEL, D_MODEL + SWA_KV
            q = _dot(ob, wqkv_ref[:, :kcol]) + bqkv_ref[:, :kcol]
            q_ref[r, :] = (q * (SWA_HEAD_DIM ** -0.5 * LOG2E)).astype(BF16)
            k_ref[r, :] = (_dot(ob, wqkv_ref[:, kcol:vcol]) + bqkv_ref[:, kcol:vcol]).astype(BF16)
            v_ref[r, :] = (_dot(ob, wqkv_ref[:, vcol:]) + bqkv_ref[:, vcol:]).astype(BF16)

    n_ff = D_FF // FF_TILE
    y_a = out_proj(rows[0])
    y_b = out_proj(rows[1])
    h1_a = _layer_norm(y_a, g1_ref[...], b1_ref[...])
    h1b_a = h1_a.astype(BF16)
    acc_a = mlp(h1b_a, DEEPNORM_ALPHA * h1_a, range(0, 1))
    h1_b = _layer_norm(y_b, g1_ref[...], b1_ref[...])
    h1b_b = h1_b.astype(BF16)
    acc_a = mlp(h1b_a, acc_a, range(1, n_ff))
    acc_b = mlp(h1b_b, DEEPNORM_ALPHA * h1_b, range(0, 1))
    h2_a = _layer_norm(acc_a, g2_ref[...], b2_ref[...])
    acc_b = mlp(h1b_b, acc_b, range(1, n_ff))
    h2_b = _layer_norm(acc_b, g2_ref[...], b2_ref[...])
    ple(rows[0], h2_a)
    ple(rows[1], h2_b)


def _post_mixer(mix, h2d, p3, layer, weights, out_bias=None, qkv=None):
    t = h2d.shape[0]
    row = lambda w: pl.BlockSpec((POST_TILE, w), lambda i: (i, 0))
    wo, rest = weights[0], tuple(weights[1:])
    consts = (wo,) + ((out_bias,) if out_bias is not None else ()) + rest + (qkv or ())
    out_specs = [row(D_MODEL)]
    out_shape = [jax.ShapeDtypeStruct((t, D_MODEL), F32)]
    if qkv is not None:
        out_specs += [row(D_MODEL), row(SWA_KV), row(SWA_KV)]
        out_shape += [jax.ShapeDtypeStruct((t, D_MODEL), BF16),
                      jax.ShapeDtypeStruct((t, SWA_KV), BF16),
                      jax.ShapeDtypeStruct((t, SWA_KV), BF16)]
    return pl.pallas_call(
        functools.partial(_post_mixer_kernel, layer=layer, has_out_bias=out_bias is not None,
                          next_qkv=qkv is not None),
        grid=(t // POST_TILE,),
        in_specs=[row(D_MODEL), row(D_MODEL),
                  pl.BlockSpec((None, POST_TILE, PLE_DIM), lambda i: (layer, i, 0))]
                 + [_resident(w.shape, layer if w.ndim == 3 else None) for w in consts],
        out_specs=out_specs,
        out_shape=out_shape,
        compiler_params=pltpu.CompilerParams(
            dimension_semantics=("parallel",), vmem_limit_bytes=VMEM_LIMIT),
        name=f"post_mixer_{layer}",
    )(mix, h2d, p3, *consts)


def kernel(x, p, gla_w_in, gla_w_gk_up, gla_b_gk, gla_norm_g, gla_w_out, swa_w_qkv, swa_b_qkv,
           swa_sinks, swa_w_out, swa_b_out, mlp_w_up, mlp_w_down, ln1_g, ln1_b, ln2_g, ln2_b,
           ple_w_proj, ple_w_gate, ple_b_gate):
    batch, seq, d = x.shape
    t = batch * seq
    x2 = x.reshape(t, d)
    p3 = p.reshape(DEPTH, t, PLE_DIM)
    vec = lambda a: a.reshape(1, -1).astype(F32)

    flat = lambda w: w.reshape(-1, w.shape[-1])
    norm_g = vec(jnp.tile(gla_norm_g[0], GLA_HEADS))
    mix, w_up, w_down, w_gate, w_proj, w_out0, w_out1, w_qkv = _gla(
        x2, gla_w_in[0].astype(BF16), gla_w_gk_up[0], vec(gla_b_gk[0]), norm_g, seq,
        cast=[flat(mlp_w_up), flat(mlp_w_down), flat(ple_w_gate), flat(ple_w_proj),
              gla_w_out[0], swa_w_out[0], swa_w_qkv[0]])
    w_up, w_down = w_up.reshape(mlp_w_up.shape), w_down.reshape(mlp_w_down.shape)
    w_gate, w_proj = w_gate.reshape(ple_w_gate.shape), w_proj.reshape(ple_w_proj.shape)

    def post_weights(wo):
        return (wo, ln1_g, ln1_b, w_up, w_down, ln2_g, ln2_b, w_gate, ple_b_gate, w_proj)

    h, q, k, v = _post_mixer(mix, x2, p3, 0, post_weights(w_out0),
                             qkv=(w_qkv, vec(swa_b_qkv[0])))

    mix = _swa_attn(swa_sinks[0].astype(F32), q, k, v, batch, seq)
    (h,) = _post_mixer(mix, h, p3, 1, post_weights(w_out1), out_bias=vec(swa_b_out[0]))
    return h.reshape(batch, seq, d)
```
